```python
import jax, jax.numpy as jnp
from jax import lax
import numpy as np

D_MODEL = 2048
BATCH = 4
SEQ = 8192
DEPTH = 4

N_MIXERS = 3
N_A = (DEPTH + 2) // N_MIXERS
N_B = (DEPTH + 1) // N_MIXERS
N_C = DEPTH // N_MIXERS
SHORT_CONV_WIDTH = 3
POOL_WINDOWS = (2, 4, 8, 16)
N_POOL_GROUPS = len(POOL_WINDOWS)
POOL_GROUP_DIM = D_MODEL // N_POOL_GROUPS
CONFORMER_CONV_WIDTH = 31
FFN_CONV_WIDTH = 3
FF_MULTIPLE = 256
D_FF = FF_MULTIPLE * (-(-(2 * 4 * D_MODEL // 3) // FF_MULTIPLE))
RMS_EPS = 1e-5
LN_EPS = 1e-5

kernel_name = "hybrid_shortconv_pool_conformer_trunk"


def rms_norm(x, g):
    xf = x.astype(jnp.float32)
    y = xf * lax.rsqrt(jnp.mean(xf * xf, axis=-1, keepdims=True) + RMS_EPS)
    return (y * g.astype(jnp.float32)).astype(x.dtype)


def layer_norm(x, g, b):
    xf = x.astype(jnp.float32)
    mu = jnp.mean(xf, axis=-1, keepdims=True)
    xc = xf - mu
    var = jnp.mean(xc * xc, axis=-1, keepdims=True)
    y = xc * lax.rsqrt(var + LN_EPS) * g.astype(jnp.float32) + b.astype(jnp.float32)
    return y.astype(x.dtype)


def causal_depthwise_conv(x, w):
    k, c = w.shape
    return lax.conv_general_dilated(
        x, w[:, None, :].astype(x.dtype), window_strides=(1,), padding=((k - 1, 0),),
        dimension_numbers=("NWC", "WIO", "NWC"), feature_group_count=c)


def short_conv_mixer(h, w_in, conv_w, w_out):
    b_gate, c_gate, v = jnp.split(h @ w_in, 3, axis=-1)
    u = causal_depthwise_conv(c_gate * v, conv_w)
    return (b_gate * u) @ w_out


def pooling_mixer(h, w_group, scale):
    bsz, s, d = h.shape
    hf = h.astype(jnp.float32)
    csum = jnp.cumsum(hf, axis=1)
    count = jnp.arange(1, s + 1, dtype=jnp.float32)[None, :, None]
    outs = []
    for g, win in enumerate(POOL_WINDOWS):
        sl = slice(g * POOL_GROUP_DIM, (g + 1) * POOL_GROUP_DIM)
        cg = csum[..., sl]
        lag = jnp.pad(cg, ((0, 0), (win, 0), (0, 0)))[:, :s]
        mean = (cg - lag) / jnp.minimum(count, float(win))
        outs.append(mean - hf[..., sl])
    pooled = jnp.stack(outs, axis=2).astype(h.dtype)
    mixed = jnp.einsum("bsgi,gio->bsgo", pooled, w_group).reshape(bsz, s, d)
    return mixed * scale


def conformer_conv_mixer(h, w_pw1, b_pw1, conv_w, conv_b, ln_g, ln_b, w_pw2, b_pw2):
    a, gate = jnp.split(h @ w_pw1 + b_pw1, 2, axis=-1)
    u = a * jax.nn.sigmoid(gate)
    u = causal_depthwise_conv(u, conv_w) + conv_b
    u = layer_norm(u, ln_g, ln_b)
    return jax.nn.silu(u) @ w_pw2 + b_pw2


def conv_glu_ffn(h, w_gate, w_up, conv_w, conv_b, w_down):
    a = causal_depthwise_conv(h @ w_gate, conv_w) + conv_b
    return (jax.nn.silu(a) * (h @ w_up)) @ w_down


def setup_inputs(seed: int = 0) -> dict:
    key = jax.random.key(seed)
    ks = jax.random.split(key, 32)
    D, F = D_MODEL, D_FF
    nrm = lambda k, shape, fan: jax.random.normal(k, shape, jnp.float32) * (fan ** -0.5)
    gain = lambda k, shape: 1.0 + 0.02 * jax.random.normal(k, shape, jnp.float32)
    small = lambda k, shape: 0.01 * jax.random.normal(k, shape, jnp.float32)
    return {
        "x": jax.random.normal(ks[0], (BATCH, SEQ, D), jnp.float32),
        "mix_norm_g": gain(ks[1], (DEPTH, D)),
        "ffn_norm_g": gain(ks[2], (DEPTH, D)),
        "a_w_in": nrm(ks[3], (N_A, D, 3 * D), D),
        "a_conv_w": nrm(ks[4], (N_A, SHORT_CONV_WIDTH, D), SHORT_CONV_WIDTH),
        "a_w_out": nrm(ks[5], (N_A, D, D), D),
        "b_w_group": nrm(ks[6], (N_B, N_POOL_GROUPS, POOL_GROUP_DIM, POOL_GROUP_DIM), POOL_GROUP_DIM),
        "b_scale": 1.0 + 0.1 * jax.random.normal(ks[7], (N_B, D), jnp.float32),
        "c_w_pw1": nrm(ks[8], (N_C, D, 2 * D), D),
        "c_b_pw1": small(ks[9], (N_C, 2 * D)),
        "c_conv_w": nrm(ks[10], (N_C, CONFORMER_CONV_WIDTH, D), CONFORMER_CONV_WIDTH),
        "c_conv_b": small(ks[11], (N_C, D)),
        "c_ln_g": gain(ks[12], (N_C, D)),
        "c_ln_b": small(ks[13], (N_C, D)),
        "c_w_pw2": nrm(ks[14], (N_C, D, D), D),
        "c_b_pw2": small(ks[15], (N_C, D)),
        "f_w_gate": nrm(ks[16], (DEPTH, D, F), D),
        "f_w_up": nrm(ks[17], (DEPTH, D, F), D),
        "f_conv_w": nrm(ks[18], (DEPTH, FFN_CONV_WIDTH, F), FFN_CONV_WIDTH),
        "f_conv_b": small(ks[19], (DEPTH, F)),
        "f_w_down": nrm(ks[20], (DEPTH, F, D), F),
        "final_norm_g": gain(ks[21], (D,)),
    }


def reference(x, mix_norm_g, ffn_norm_g, a_w_in, a_conv_w, a_w_out, b_w_group, b_scale,
              c_w_pw1, c_b_pw1, c_conv_w, c_conv_b, c_ln_g, c_ln_b, c_w_pw2, c_b_pw2,
              f_w_gate, f_w_up, f_conv_w, f_conv_b, f_w_down, final_norm_g):
    for i in range(DEPTH):
        kind, j = i % N_MIXERS, i // N_MIXERS
        h = rms_norm(x, mix_norm_g[i])
        if kind == 0:
            y = short_conv_mixer(h, a_w_in[j], a_conv_w[j], a_w_out[j])
        elif kind == 1:
            y = pooling_mixer(h, b_w_group[j], b_scale[j])
        else:
            y = conformer_conv_mixer(h, c_w_pw1[j], c_b_pw1[j], c_conv_w[j], c_conv_b[j],
                                     c_ln_g[j], c_ln_b[j], c_w_pw2[j], c_b_pw2[j])
        x = x + y
        h = rms_norm(x, ffn_norm_g[i])
        x = x + conv_glu_ffn(h, f_w_gate[i], f_w_up[i], f_conv_w[i], f_conv_b[i], f_w_down[i])
    return rms_norm(x, final_norm_g)
```

```python
import functools

import jax
import jax.numpy as jnp
from jax import lax
from jax.experimental import pallas as pl
from jax.experimental.pallas import tpu as pltpu

N_MIXERS = 3
POOL_WINDOWS = (2, 4, 8, 16)
RMS_EPS = 1e-5
LN_EPS = 1e-5

MXU_DTYPE = jnp.bfloat16
SUBLANES = 8
LANES = 128
MAX_SEQ_TILE = 512
MAX_CHANNEL_TILE = 512
CONV_ROW_BLOCK = 64
VMEM_LIMIT_BYTES = 56 * 1024 * 1024


def _largest_tile(n, cap, multiple):
    if n <= cap:
        return n
    t = cap - cap % multiple
    while t >= multiple:
        if n % t == 0:
            return t
        t -= multiple
    raise ValueError(f"no tile of {n} that is a multiple of {multiple} and <= {cap}")


def _compiler_params(n_grid_dims):
    return pltpu.CompilerParams(
        dimension_semantics=("arbitrary",) * n_grid_dims,
        vmem_limit_bytes=VMEM_LIMIT_BYTES,
    )


def _rms_norm(x, g):
    return x * lax.rsqrt(jnp.mean(x * x, axis=-1, keepdims=True) + RMS_EPS) * g


def _dot(a, b):
    return jnp.dot(a, b, preferred_element_type=jnp.float32)


def _delay_rows(cur, prev_tail, k):
    rolled = pltpu.roll(cur, k, axis=0)
    tail = pltpu.roll(prev_tail, k, axis=0)
    rows = lax.broadcasted_iota(jnp.int32, prev_tail.shape, 0)
    head = jnp.where(rows < k, tail, rolled[:SUBLANES])
    return jnp.concatenate([head, rolled[SUBLANES:]], axis=0)


def _causal_conv3(cur, prev_tail, w):
    return (w[2:3] * cur + w[1:2] * _delay_rows(cur, prev_tail, 1)
            + w[0:1] * _delay_rows(cur, prev_tail, 2))


def _take_carry(carry_sc, j, seq_tile_idx, new_tail):
    @pl.when(seq_tile_idx == 0)
    def _():
        carry_sc[j] = jnp.zeros(carry_sc.shape[1:], carry_sc.dtype)

    prev = carry_sc[j]
    carry_sc[j] = new_tail
    return prev


def _ffn_kernel(x_ref, g_ref, wg_ref, wu_ref, cw_ref, cb_ref, wd_ref, fg_ref, o_ref,
                h_sc, carry_sc, *, final_norm):
    s, j = pl.program_id(1), pl.program_id(2)
    tm = x_ref.shape[0]

    @pl.when(j == 0)
    def _():
        x = x_ref[...]
        h_sc[...] = _rms_norm(x, g_ref[...]).astype(h_sc.dtype)
        o_ref[...] = x

    h = h_sc[...]
    gate = _dot(h, wg_ref[...])
    up = _dot(h, wu_ref[...])
    prev = _take_carry(carry_sc, j, s, gate[tm - SUBLANES:])
    a = _causal_conv3(gate, prev, cw_ref[...]) + cb_ref[...]
    act = (a * jax.nn.sigmoid(a) * up).astype(wd_ref.dtype)
    o_ref[...] += _dot(act, wd_ref[...])

    if final_norm:
        @pl.when(j == pl.num_programs(2) - 1)
        def _():
            o_ref[...] = _rms_norm(o_ref[...], fg_ref[...])


def _ffn_block(x, norm_g, w_gate, w_up, conv_w, conv_b, w_down, final_g, final_norm):
    b, s, d = x.shape
    f = w_gate.shape[1]
    tm = _largest_tile(s, MAX_SEQ_TILE, 2 * SUBLANES)
    tf = _largest_tile(f, MAX_CHANNEL_TILE, LANES)
    nj = f // tf
    row = lambda v: v.reshape(1, -1)
    return pl.pallas_call(
        functools.partial(_ffn_kernel, final_norm=final_norm),
        out_shape=jax.ShapeDtypeStruct(x.shape, x.dtype),
        grid=(b, s // tm, nj),
        in_specs=[
            pl.BlockSpec((None, tm, d), lambda bi, si, j: (bi, si, 0)),
            pl.BlockSpec((1, d), lambda bi, si, j: (0, 0)),
            pl.BlockSpec((d, tf), lambda bi, si, j: (0, j)),
            pl.BlockSpec((d, tf), lambda bi, si, j: (0, j)),
            pl.BlockSpec((conv_w.shape[0], tf), lambda bi, si, j: (0, j)),
            pl.BlockSpec((1, tf), lambda bi, si, j: (0, j)),
            pl.BlockSpec((tf, d), lambda bi, si, j: (j, 0)),
            pl.BlockSpec((1, d), lambda bi, si, j: (0, 0)),
        ],
        out_specs=pl.BlockSpec((None, tm, d), lambda bi, si, j: (bi, si, 0)),
        scratch_shapes=[
            pltpu.VMEM((tm, d), w_gate.dtype),
            pltpu.VMEM((nj, SUBLANES, tf), jnp.float32),
        ],
        compiler_params=_compiler_params(3),
        name="conv_glu_ffn",
    )(x, row(norm_g), w_gate, w_up, conv_w, row(conv_b), w_down, row(final_g))


def _short_conv_kernel(x_ref, g_ref, wb_ref, wc_ref, wv_ref, cw_ref, wo_ref, o_ref,
                       h_sc, carry_sc):
    s, j = pl.program_id(1), pl.program_id(2)
    tm = x_ref.shape[0]

    @pl.when(j == 0)
    def _():
        x = x_ref[...]
        h_sc[...] = _rms_norm(x, g_ref[...]).astype(h_sc.dtype)
        o_ref[...] = x

    h = h_sc[...]
    cv = _dot(h, wc_ref[...]) * _dot(h, wv_ref[...])
    prev = _take_carry(carry_sc, j, s, cv[tm - SUBLANES:])
    u = _causal_conv3(cv, prev, cw_ref[...])
    z = (_dot(h, wb_ref[...]) * u).astype(wo_ref.dtype)
    o_ref[...] += _dot(z, wo_ref[...])


def _short_conv_block(x, norm_g, w_in, conv_w, w_out):
    b, s, d = x.shape
    tm = _largest_tile(s, MAX_SEQ_TILE, 2 * SUBLANES)
    tc = _largest_tile(d, MAX_CHANNEL_TILE, LANES)
    nj = d // tc
    return pl.pallas_call(
        _short_conv_kernel,
        out_shape=jax.ShapeDtypeStruct(x.shape, x.dtype),
        grid=(b, s // tm, nj),
        in_specs=[
            pl.BlockSpec((None, tm, d), lambda bi, si, j: (bi, si, 0)),
            pl.BlockSpec((1, d), lambda bi, si, j: (0, 0)),
            pl.BlockSpec((d, tc), lambda bi, si, j: (0, j)),
            pl.BlockSpec((d, tc), lambda bi, si, j: (0, j + nj)),
            pl.BlockSpec((d, tc), lambda bi, si, j: (0, j + 2 * nj)),
            pl.BlockSpec((conv_w.shape[0], tc), lambda bi, si, j: (0, j)),
            pl.BlockSpec((tc, d), lambda bi, si, j: (j, 0)),
        ],
        out_specs=pl.BlockSpec((None, tm, d), lambda bi, si, j: (bi, si, 0)),
        scratch_shapes=[
            pltpu.VMEM((tm, d), w_in.dtype),
            pltpu.VMEM((nj, SUBLANES, tc), jnp.float32),
        ],
        compiler_params=_compiler_params(3),
        name="short_conv_mixer",
    )(x, norm_g.reshape(1, -1), w_in, w_in, w_in, conv_w, w_out)


def _pooling_kernel(x_ref, g_ref, wgrp_ref, scale_ref, o_ref, carry_sc):
    s = pl.program_id(1)
    tm = x_ref.shape[0]
    hist = carry_sc.shape[0]
    dg = wgrp_ref.shape[1]

    @pl.when(s == 0)
    def _():
        carry_sc[...] = jnp.zeros(carry_sc.shape, carry_sc.dtype)

    x = x_ref[...]
    h = _rms_norm(x, g_ref[...])
    prev = carry_sc[...]
    carry_sc[...] = h[tm - hist:]
    seen = (s * tm + 1 + lax.broadcasted_iota(jnp.int32, (tm, 1), 0)).astype(jnp.float32)
    for gi, win in enumerate(POOL_WINDOWS):
        sl = slice(gi * dg, (gi + 1) * dg)
        hg = h[:, sl]
        acc = jnp.concatenate([prev[:, sl], hg], axis=0)
        step = 1
        while step < win:
            acc = acc + pltpu.roll(acc, step, axis=0)
            step *= 2
        pooled = acc[hist:] * (1.0 / jnp.minimum(seen, float(win))) - hg
        mixed = _dot(pooled.astype(wgrp_ref.dtype), wgrp_ref[gi])
        o_ref[:, sl] = x[:, sl] + mixed * scale_ref[:, sl]


def _pooling_block(x, norm_g, w_group, scale):
    b, s, d = x.shape
    n_groups, dg, _ = w_group.shape
    assert n_groups == len(POOL_WINDOWS) and n_groups * dg == d
    hist = max(POOL_WINDOWS)
    tm = _largest_tile(s, MAX_SEQ_TILE, hist)
    return pl.pallas_call(
        _pooling_kernel,
        out_shape=jax.ShapeDtypeStruct(x.shape, x.dtype),
        grid=(b, s // tm),
        in_specs=[
            pl.BlockSpec((None, tm, d), lambda bi, si: (bi, si, 0)),
            pl.BlockSpec((1, d), lambda bi, si: (0, 0)),
            pl.BlockSpec((n_groups, dg, dg), lambda bi, si: (0, 0, 0)),
            pl.BlockSpec((1, d), lambda bi, si: (0, 0)),
        ],
        out_specs=pl.BlockSpec((None, tm, d), lambda bi, si: (bi, si, 0)),
        scratch_shapes=[pltpu.VMEM((hist, d), jnp.float32)],
        compiler_params=_compiler_params(2),
        name="pooling_mixer",
    )(x, norm_g.reshape(1, -1), w_group, scale.reshape(1, -1))


def _conformer_kernel(x_ref, g_ref, wa_ref, wgt_ref, ba_ref, bgt_ref, cw_ref, cb_ref,
                      lng_ref, lnb_ref, w2_ref, b2_ref, o_ref,
                      h_sc, u_sc, act_sc, ext_sc, carry_sc):
    s, j = pl.program_id(1), pl.program_id(2)
    nc = pl.num_programs(2) // 2
    tm = x_ref.shape[0]
    tc = wa_ref.shape[1]
    taps = cw_ref.shape[0]
    hist = carry_sc.shape[1]

    @pl.when(j == 0)
    def _():
        h_sc[...] = _rms_norm(x_ref[...], g_ref[...]).astype(h_sc.dtype)

    @pl.when(j < nc)
    def _():
        h = h_sc[...]
        a = _dot(h, wa_ref[...]) + ba_ref[...]
        gt = _dot(h, wgt_ref[...]) + bgt_ref[...]
        u = a * jax.nn.sigmoid(gt)
        ext_sc[:hist] = _take_carry(carry_sc, j, s, u[tm - hist:])
        ext_sc[hist:] = u
        col0 = pl.multiple_of(j * tc, tc)
        for r0 in range(0, tm, CONV_ROW_BLOCK):
            for l0 in range(0, tc, LANES):
                acc = jnp.broadcast_to(cb_ref[:, l0:l0 + LANES], (CONV_ROW_BLOCK, LANES))
                for k in range(taps):
                    start = hist + r0 - (taps - 1 - k)
                    acc = acc + (cw_ref[k:k + 1, l0:l0 + LANES]
                                 * ext_sc[start:start + CONV_ROW_BLOCK, l0:l0 + LANES])
                u_sc[r0:r0 + CONV_ROW_BLOCK, pl.ds(col0 + l0, LANES)] = acc

    @pl.when(j == nc)
    def _():
        u = u_sc[...]
        mu = jnp.mean(u, axis=-1, keepdims=True)
        uc = u - mu
        var = jnp.mean(uc * uc, axis=-1, keepdims=True)
        y = uc * lax.rsqrt(var + LN_EPS) * lng_ref[...] + lnb_ref[...]
        act_sc[...] = (y * jax.nn.sigmoid(y)).astype(act_sc.dtype)

    @pl.when(j >= nc)
    def _():
        col0 = pl.multiple_of((j - nc) * tc, tc)
        o_ref[...] = (x_ref[:, pl.ds(col0, tc)] + _dot(act_sc[...], w2_ref[...]) + b2_ref[...])


def _conformer_block(x, norm_g, w_pw1, b_pw1, conv_w, conv_b, ln_g, ln_b, w_pw2, b_pw2):
    b, s, d = x.shape
    taps = conv_w.shape[0]
    hist = -(-(taps - 1) // SUBLANES) * SUBLANES
    tm = _largest_tile(s, MAX_SEQ_TILE, CONV_ROW_BLOCK)
    tc = _largest_tile(d, MAX_CHANNEL_TILE, LANES)
    nc = d // tc
    assert tm % CONV_ROW_BLOCK == 0 and tm >= hist
    row = lambda v: v.reshape(1, -1)
    first = lambda j: jnp.minimum(j, nc - 1)
    second = lambda j: jnp.maximum(j - nc, 0)
    return pl.pallas_call(
        _conformer_kernel,
        out_shape=jax.ShapeDtypeStruct(x.shape, x.dtype),
        grid=(b, s // tm, 2 * nc),
        in_specs=[
            pl.BlockSpec((None, tm, d), lambda bi, si, j: (bi, si, 0)),
            pl.BlockSpec((1, d), lambda bi, si, j: (0, 0)),
            pl.BlockSpec((d, tc), lambda bi, si, j: (0, first(j))),
            pl.BlockSpec((d, tc), lambda bi, si, j: (0, first(j) + nc)),
            pl.BlockSpec((1, tc), lambda bi, si, j: (0, first(j))),
            pl.BlockSpec((1, tc), lambda bi, si, j: (0, first(j) + nc)),
            pl.BlockSpec((taps, tc), lambda bi, si, j: (0, first(j))),
            pl.BlockSpec((1, tc), lambda bi, si, j: (0, first(j))),
            pl.BlockSpec((1, d), lambda bi, si, j: (0, 0)),
            pl.BlockSpec((1, d), lambda bi, si, j: (0, 0)),
            pl.BlockSpec((d, tc), lambda bi, si, j: (0, second(j))),
            pl.BlockSpec((1, tc), lambda bi, si, j: (0, second(j))),
        ],
        out_specs=pl.BlockSpec((None, tm, tc), lambda bi, si, j: (bi, si, second(j))),
        scratch_shapes=[
            pltpu.VMEM((tm, d), w_pw1.dtype),
            pltpu.VMEM((tm, d), jnp.float32),
            pltpu.VMEM((tm, d), w_pw2.dtype),
            pltpu.VMEM((tm + hist, tc), jnp.float32),
            pltpu.VMEM((nc, hist, tc), jnp.float32),
        ],
        compiler_params=_compiler_params(3),
        name="conformer_mixer",
    )(x, row(norm_g), w_pw1, w_pw1, row(b_pw1), row(b_pw1), conv_w, row(conv_b),
      row(ln_g), row(ln_b), w_pw2, row(b_pw2))


def kernel(x, mix_norm_g, ffn_norm_g, a_w_in, a_conv_w, a_w_out, b_w_group, b_scale, c_w_pw1, c_b_pw1, c_conv_w, c_conv_b, c_ln_g, c_ln_b, c_w_pw2, c_b_pw2, f_w_gate, f_w_up, f_conv_w, f_conv_b, f_w_down, final_norm_g):
    mxu = MXU_DTYPE
    depth = mix_norm_g.shape[0]
    for i in range(depth):
        kind, j = i % N_MIXERS, i // N_MIXERS
        if kind == 0:
            x = _short_conv_block(x, mix_norm_g[i], a_w_in[j].astype(mxu), a_conv_w[j],
                                  a_w_out[j].astype(mxu))
        elif kind == 1:
            x = _pooling_block(x, mix_norm_g[i], b_w_group[j].astype(mxu), b_scale[j])
        else:
            x = _conformer_block(x, mix_norm_g[i], c_w_pw1[j].astype(mxu), c_b_pw1[j],
                                 c_conv_w[j], c_conv_b[j], c_ln_g[j], c_ln_b[j],
                                 c_w_pw2[j].astype(mxu), c_b_pw2[j])
        x = _ffn_block(x, ffn_norm_g[i], f_w_gate[i].astype(mxu), f_w_up[i].astype(mxu),
                       f_conv_w[i], f_conv_b[i], f_w_down[i].astype(mxu), final_norm_g,
                       final_norm=(i == depth - 1))
    return x
```

```python
import functools

import jax
import jax.numpy as jnp
from jax import lax
from jax.experimental import pallas as pl
from jax.experimental.pallas import tpu as pltpu

N_MIXERS = 3
POOL_WINDOWS = (2, 4, 8, 16)
RMS_EPS = 1e-5
LN_EPS = 1e-5

MXU_DTYPE = jnp.bfloat16
SUBLANES = 8
LANES = 128
MAX_SEQ_TILE = 512
MAX_CHANNEL_TILE = 512
CONV_ROW_BLOCK = 64
VMEM_LIMIT_BYTES = 56 * 1024 * 1024


def _largest_tile(n, cap, multiple):
    if n <= cap:
        return n
    t = cap - cap % multiple
    while t >= multiple:
        if n % t == 0:
            return t
        t -= multiple
    raise ValueError(f"no tile of {n} that is a multiple of {multiple} and <= {cap}")


def _compiler_params(n_grid_dims):
    return pltpu.CompilerParams(
        dimension_semantics=("arbitrary",) * n_grid_dims,
        vmem_limit_bytes=VMEM_LIMIT_BYTES,
    )


def _rms_norm(x, g):
    return x * lax.rsqrt(jnp.mean(x * x, axis=-1, keepdims=True) + RMS_EPS) * g


def _dot(a, b):
    return jnp.dot(a, b, preferred_element_type=jnp.float32)


def _delay_rows(cur, prev_tail, k):
    rolled = pltpu.roll(cur, k, axis=0)
    tail = pltpu.roll(prev_tail, k, axis=0)
    rows = lax.broadcasted_iota(jnp.int32, prev_tail.shape, 0)
    head = jnp.where(rows < k, tail, rolled[:SUBLANES])
    return jnp.concatenate([head, rolled[SUBLANES:]], axis=0)


def _causal_conv3(cur, prev_tail, w):
    return (w[2:3] * cur + w[1:2] * _delay_rows(cur, prev_tail, 1)
            + w[0:1] * _delay_rows(cur, prev_tail, 2))


def _take_carry(carry_sc, j, seq_tile_idx, new_tail):
    @pl.when(seq_tile_idx == 0)
    def _():
        carry_sc[j] = jnp.zeros(carry_sc.shape[1:], carry_sc.dtype)

    prev = carry_sc[j]
    carry_sc[j] = new_tail
    return prev


class _Steps:
    def __init__(self, nj, tiles_per_seq, n_tiles):
        assert nj >= 2, "a tile's input block must still be resident at its first down step"
        self.nj, self.tiles_per_seq, self.n_tiles = nj, tiles_per_seq, n_tiles
        self.n_items = n_tiles * nj
        self.n_steps = self.n_items + 1

    def chunk(self, t):
        return t % self.nj

    def down_chunk(self, t):
        return (t + self.nj - 1) % self.nj

    def tile(self, t):
        return jnp.minimum(t, self.n_items - 1) // self.nj

    def down_tile(self, t):
        return jnp.maximum(t - 1, 0) // self.nj

    def tile_block(self, tile):
        return (tile // self.tiles_per_seq, tile % self.tiles_per_seq, 0)


def _run_step(steps, x_ref, g_ref, h_sc, act_bufs, carry_sc, o_ref, gate_fn, down_w_ref):
    t = pl.program_id(0)
    j = steps.chunk(t)

    @pl.when(t == 0)
    def _():
        for ref in (*act_bufs, carry_sc):
            ref[...] = jnp.zeros(ref.shape, ref.dtype)

    @pl.when((j == 0) & (t < steps.n_items))
    def _():
        h_sc[...] = _rms_norm(x_ref[...], g_ref[...]).astype(h_sc.dtype)

    seq_start = steps.tile(t) % steps.tiles_per_seq == 0
    first_down = (steps.down_chunk(t) == 0) | (t == 0)

    def body(act_out, act_in):
        carry_in = jnp.where(seq_start, 0.0, carry_sc[j])
        act, carry_out = gate_fn(h_sc[...], carry_in)
        carry_sc[j] = carry_out
        act_out[...] = act.astype(act_out.dtype)
        o_ref[...] = (jnp.where(first_down, x_ref[...], o_ref[...])
                      + _dot(act_in[...], down_w_ref[...]))

    for parity in (0, 1):
        pl.when(t % 2 == parity)(
            functools.partial(body, act_bufs[parity], act_bufs[1 - parity]))
    return t


def _ffn_kernel(x_ref, g_ref, wg_ref, wu_ref, cw_ref, cb_ref, wd_ref, fg_ref, o_ref,
                h_sc, act_a, act_b, carry_sc, *, steps, final_norm):
    tm = x_ref.shape[0]

    def gate_fn(h, carry_in):
        gate = _dot(h, wg_ref[...])
        up = _dot(h, wu_ref[...])
        a = _causal_conv3(gate, carry_in, cw_ref[...]) + cb_ref[...]
        return a * jax.nn.sigmoid(a) * up, gate[tm - SUBLANES:]

    t = _run_step(steps, x_ref, g_ref, h_sc, (act_a, act_b), carry_sc, o_ref, gate_fn, wd_ref)

    if final_norm:
        @pl.when((steps.down_chunk(t) == steps.nj - 1) & (t > 0))
        def _():
            o_ref[...] = _rms_norm(o_ref[...], fg_ref[...])


def _short_conv_kernel(x_ref, g_ref, wb_ref, wc_ref, wv_ref, cw_ref, wo_ref, o_ref,
                       h_sc, act_a, act_b, carry_sc, *, steps):
    tm = x_ref.shape[0]

    def gate_fn(h, carry_in):
        cv = _dot(h, wc_ref[...]) * _dot(h, wv_ref[...])
        u = _causal_conv3(cv, carry_in, cw_ref[...])
        return _dot(h, wb_ref[...]) * u, cv[tm - SUBLANES:]

    _run_step(steps, x_ref, g_ref, h_sc, (act_a, act_b), carry_sc, o_ref, gate_fn, wo_ref)


def _lagged_call(kernel_fn, name, x, steps, tm, tc, up_weights, per_chunk_rows, down_weight,
                 full_rows):
    d = x.shape[-1]
    nj = steps.nj
    x_spec = pl.BlockSpec((None, tm, d), lambda t: steps.tile_block(steps.tile(t)))
    row_spec = pl.BlockSpec((1, d), lambda t: (0, 0))
    in_specs = [x_spec, row_spec]
    operands = [x, full_rows[0]]
    for w, off in up_weights:
        in_specs.append(pl.BlockSpec((d, tc), lambda t, off=off: (0, steps.chunk(t) + off * nj)))
        operands.append(w)
    for r in per_chunk_rows:
        in_specs.append(pl.BlockSpec((r.shape[0], tc), lambda t: (0, steps.chunk(t))))
        operands.append(r)
    in_specs.append(pl.BlockSpec((tc, d), lambda t: (steps.down_chunk(t), 0)))
    operands.append(down_weight)
    for r in full_rows[1:]:
        in_specs.append(row_spec)
        operands.append(r)
    return pl.pallas_call(
        kernel_fn,
        out_shape=jax.ShapeDtypeStruct(x.shape, x.dtype),
        grid=(steps.n_steps,),
        in_specs=in_specs,
        out_specs=pl.BlockSpec((None, tm, d), lambda t: steps.tile_block(steps.down_tile(t))),
        scratch_shapes=[pltpu.VMEM((tm, d), down_weight.dtype)]
        + [pltpu.VMEM((tm, tc), down_weight.dtype)] * 2
        + [pltpu.VMEM((nj, SUBLANES, tc), jnp.float32)],
        compiler_params=_compiler_params(1),
        name=name,
    )(*operands)


def _ffn_block(x, norm_g, w_gate, w_up, conv_w, conv_b, w_down, final_g, final_norm):
    b, s, d = x.shape
    f = w_gate.shape[1]
    tm = _largest_tile(s, MAX_SEQ_TILE, 2 * SUBLANES)
    tf = _largest_tile(f, MAX_CHANNEL_TILE, LANES)
    steps = _Steps(f // tf, s // tm, b * (s // tm))
    row = lambda v: v.reshape(1, -1)
    return _lagged_call(
        functools.partial(_ffn_kernel, steps=steps, final_norm=final_norm), "conv_glu_ffn",
        x, steps, tm, tf, [(w_gate, 0), (w_up, 0)], [conv_w, row(conv_b)], w_down,
        [row(norm_g), row(final_g)])


def _short_conv_block(x, norm_g, w_in, conv_w, w_out):
    b, s, d = x.shape
    tm = _largest_tile(s, MAX_SEQ_TILE, 2 * SUBLANES)
    tc = _largest_tile(d, MAX_CHANNEL_TILE, LANES)
    steps = _Steps(d // tc, s // tm, b * (s // tm))
    return _lagged_call(
        functools.partial(_short_conv_kernel, steps=steps), "short_conv_mixer",
        x, steps, tm, tc, [(w_in, 0), (w_in, 1), (w_in, 2)], [conv_w], w_out,
        [norm_g.reshape(1, -1)])


def _pooling_kernel(x_ref, g_ref, wgrp_ref, scale_ref, o_ref, carry_sc):
    s = pl.program_id(1)
    tm = x_ref.shape[0]
    hist = carry_sc.shape[0]
    dg = wgrp_ref.shape[1]

    @pl.when(s == 0)
    def _():
        carry_sc[...] = jnp.zeros(carry_sc.shape, carry_sc.dtype)

    x = x_ref[...]
    h = _rms_norm(x, g_ref[...])
    prev = carry_sc[...]
    carry_sc[...] = h[tm - hist:]
    seen = (s * tm + 1 + lax.broadcasted_iota(jnp.int32, (tm, 1), 0)).astype(jnp.float32)
    for gi, win in enumerate(POOL_WINDOWS):
        sl = slice(gi * dg, (gi + 1) * dg)
        hg = h[:, sl]
        acc = jnp.concatenate([prev[:, sl], hg], axis=0)
        step = 1
        while step < win:
            acc = acc + pltpu.roll(acc, step, axis=0)
            step *= 2
        pooled = acc[hist:] * (1.0 / jnp.minimum(seen, float(win))) - hg
        mixed = _dot(pooled.astype(wgrp_ref.dtype), wgrp_ref[gi])
        o_ref[:, sl] = x[:, sl] + mixed * scale_ref[:, sl]


def _pooling_block(x, norm_g, w_group, scale):
    b, s, d = x.shape
    n_groups, dg, _ = w_group.shape
    assert n_groups == len(POOL_WINDOWS) and n_groups * dg == d
    hist = max(POOL_WINDOWS)
    tm = _largest_tile(s, MAX_SEQ_TILE, hist)
    return pl.pallas_call(
        _pooling_kernel,
        out_shape=jax.ShapeDtypeStruct(x.shape, x.dtype),
        grid=(b, s // tm),
        in_specs=[
            pl.BlockSpec((None, tm, d), lambda bi, si: (bi, si, 0)),
            pl.BlockSpec((1, d), lambda bi, si: (0, 0)),
            pl.BlockSpec((n_groups, dg, dg), lambda bi, si: (0, 0, 0)),
            pl.BlockSpec((1, d), lambda bi, si: (0, 0)),
        ],
        out_specs=pl.BlockSpec((None, tm, d), lambda bi, si: (bi, si, 0)),
        scratch_shapes=[pltpu.VMEM((hist, d), jnp.float32)],
        compiler_params=_compiler_params(2),
        name="pooling_mixer",
    )(x, norm_g.reshape(1, -1), w_group, scale.reshape(1, -1))


def _conformer_kernel(x_ref, g_ref, wa_ref, wgt_ref, ba_ref, bgt_ref, cw_ref, cb_ref,
                      lng_ref, lnb_ref, w2_ref, b2_ref, o_ref,
                      h_sc, u_sc, act_sc, ext_sc, carry_sc):
    s, j = pl.program_id(1), pl.program_id(2)
    nc = pl.num_programs(2) // 2
    tm = x_ref.shape[0]
    tc = wa_ref.shape[1]
    taps = cw_ref.shape[0]
    hist = carry_sc.shape[1]

    @pl.when(j == 0)
    def _():
        h_sc[...] = _rms_norm(x_ref[...], g_ref[...]).astype(h_sc.dtype)

    @pl.when(j < nc)
    def _():
        h = h_sc[...]
        a = _dot(h, wa_ref[...]) + ba_ref[...]
        gt = _dot(h, wgt_ref[...]) + bgt_ref[...]
        u = a * jax.nn.sigmoid(gt)
        ext_sc[:hist] = _take_carry(carry_sc, j, s, u[tm - hist:])
        ext_sc[hist:] = u
        col0 = pl.multiple_of(j * tc, tc)
        for r0 in range(0, tm, CONV_ROW_BLOCK):
            for l0 in range(0, tc, LANES):
                acc = jnp.broadcast_to(cb_ref[:, l0:l0 + LANES], (CONV_ROW_BLOCK, LANES))
                for k in range(taps):
                    start = hist + r0 - (taps - 1 - k)
                    acc = acc + (cw_ref[k:k + 1, l0:l0 + LANES]
                                 * ext_sc[start:start + CONV_ROW_BLOCK, l0:l0 + LANES])
                u_sc[r0:r0 + CONV_ROW_BLOCK, pl.ds(col0 + l0, LANES)] = acc

    @pl.when(j == nc)
    def _():
        u = u_sc[...]
        mu = jnp.mean(u, axis=-1, keepdims=True)
        uc = u - mu
        var = jnp.mean(uc * uc, axis=-1, keepdims=True)
        y = uc * lax.rsqrt(var + LN_EPS) * lng_ref[...] + lnb_ref[...]
        act_sc[...] = (y * jax.nn.sigmoid(y)).astype(act_sc.dtype)

    @pl.when(j >= nc)
    def _():
        col0 = pl.multiple_of((j - nc) * tc, tc)
        o_ref[...] = (x_ref[:, pl.ds(col0, tc)] + _dot(act_sc[...], w2_ref[...]) + b2_ref[...])


def _conformer_block(x, norm_g, w_pw1, b_pw1, conv_w, conv_b, ln_g, ln_b, w_pw2, b_pw2):
    b, s, d = x.shape
    taps = conv_w.shape[0]
    hist = -(-(taps - 1) // SUBLANES) * SUBLANES
    tm = _largest_tile(s, MAX_SEQ_TILE, CONV_ROW_BLOCK)
    tc = _largest_tile(d, MAX_CHANNEL_TILE, LANES)
    nc = d // tc
    assert tm % CONV_ROW_BLOCK == 0 and tm >= hist
    row = lambda v: v.reshape(1, -1)
    first = lambda j: jnp.minimum(j, nc - 1)
    second = lambda j: jnp.maximum(j - nc, 0)
    return pl.pallas_call(
        _conformer_kernel,
        out_shape=jax.ShapeDtypeStruct(x.shape, x.dtype),
        grid=(b, s // tm, 2 * nc),
        in_specs=[
            pl.BlockSpec((None, tm, d), lambda bi, si, j: (bi, si, 0)),
            pl.BlockSpec((1, d), lambda bi, si, j: (0, 0)),
            pl.BlockSpec((d, tc), lambda bi, si, j: (0, first(j))),
            pl.BlockSpec((d, tc), lambda bi, si, j: (0, first(j) + nc)),
            pl.BlockSpec((1, tc), lambda bi, si, j: (0, first(j))),
            pl.BlockSpec((1, tc), lambda bi, si, j: (0, first(j) + nc)),
            pl.BlockSpec((taps, tc), lambda bi, si, j: (0, first(j))),
            pl.BlockSpec((1, tc), lambda bi, si, j: (0, first(j))),
            pl.BlockSpec((1, d), lambda bi, si, j: (0, 0)),
            pl.BlockSpec((1, d), lambda bi, si, j: (0, 0)),
            pl.BlockSpec((d, tc), lambda bi, si, j: (0, second(j))),
            pl.BlockSpec((1, tc), lambda bi, si, j: (0, second(j))),
        ],
        out_specs=pl.BlockSpec((None, tm, tc), lambda bi, si, j: (bi, si, second(j))),
        scratch_shapes=[
            pltpu.VMEM((tm, d), w_pw1.dtype),
            pltpu.VMEM((tm, d), jnp.float32),
            pltpu.VMEM((tm, d), w_pw2.dtype),
            pltpu.VMEM((tm + hist, tc), jnp.float32),
            pltpu.VMEM((nc, hist, tc), jnp.float32),
        ],
        compiler_params=_compiler_params(3),
        name="conformer_mixer",
    )(x, row(norm_g), w_pw1, w_pw1, row(b_pw1), row(b_pw1), conv_w, row(conv_b),
      row(ln_g), row(ln_b), w_pw2, row(b_pw2))


def kernel(x, mix_norm_g, ffn_norm_g, a_w_in, a_conv_w, a_w_out, b_w_group, b_scale, c_w_pw1, c_b_pw1, c_conv_w, c_conv_b, c_ln_g, c_ln_b, c_w_pw2, c_b_pw2, f_w_gate, f_w_up, f_conv_w, f_conv_b, f_w_down, final_norm_g):
    mxu = MXU_DTYPE
    depth = mix_norm_g.shape[0]
    for i in range(depth):
        kind, j = i % N_MIXERS, i // N_MIXERS
        if kind == 0:
            x = _short_conv_block(x, mix_norm_g[i], a_w_in[j].astype(mxu), a_conv_w[j],
                                  a_w_out[j].astype(mxu))
        elif kind == 1:
            x = _pooling_block(x, mix_norm_g[i], b_w_group[j].astype(mxu), b_scale[j])
        else:
            x = _conformer_block(x, mix_norm_g[i], c_w_pw1[j].astype(mxu), c_b_pw1[j],
                                 c_conv_w[j], c_conv_b[j], c_ln_g[j], c_ln_b[j],
                                 c_w_pw2[j].astype(mxu), c_b_pw2[j])
        x = _ffn_block(x, ffn_norm_g[i], f_w_gate[i].astype(mxu), f_w_up[i].astype(mxu),
                       f_conv_w[i], f_conv_b[i], f_w_down[i].astype(mxu), final_norm_g,
                       final_norm=(i == depth - 1))
    return x
```

```python
import functools

import jax
import jax.numpy as jnp
from jax import lax
from jax.experimental import pallas as pl
from jax.experimental.pallas import tpu as pltpu

N_MIXERS = 3
POOL_WINDOWS = (2, 4, 8, 16)
RMS_EPS = 1e-5
LN_EPS = 1e-5

MXU_DTYPE = jnp.bfloat16
SUBLANES = 8
LANES = 128
MXU_COLUMNS = 256
MAX_SEQ_TILE = 512
MAX_CHANNEL_TILE = 512
CONV_ROW_BLOCK = 64
VMEM_LIMIT_BYTES = 56 * 1024 * 1024


def _largest_tile(n, cap, multiple):
    if n <= cap:
        return n
    t = cap - cap % multiple
    while t >= multiple:
        if n % t == 0:
            return t
        t -= multiple
    raise ValueError(f"no tile of {n} that is a multiple of {multiple} and <= {cap}")


def _compiler_params(n_grid_dims):
    return pltpu.CompilerParams(
        dimension_semantics=("arbitrary",) * n_grid_dims,
        vmem_limit_bytes=VMEM_LIMIT_BYTES,
    )


def _rms_norm(x, g):
    return x * lax.rsqrt(jnp.mean(x * x, axis=-1, keepdims=True) + RMS_EPS) * g


def _dot(a, b):
    return jnp.dot(a, b, preferred_element_type=jnp.float32)


def _delay_rows(cur, prev_tail, k):
    rolled = pltpu.roll(cur, k, axis=0)
    tail = pltpu.roll(prev_tail, k, axis=0)
    rows = lax.broadcasted_iota(jnp.int32, prev_tail.shape, 0)
    head = jnp.where(rows < k, tail, rolled[:SUBLANES])
    return jnp.concatenate([head, rolled[SUBLANES:]], axis=0)


def _causal_conv3(cur, prev_tail, w):
    return (w[2:3] * cur + w[1:2] * _delay_rows(cur, prev_tail, 1)
            + w[0:1] * _delay_rows(cur, prev_tail, 2))


class _Steps:
    def __init__(self, nj, tiles_per_seq, n_tiles):
        assert nj >= 2, "a tile's input block must still be resident at its first down step"
        self.nj, self.tiles_per_seq, self.n_tiles = nj, tiles_per_seq, n_tiles
        self.n_items = n_tiles * nj
        self.n_steps = self.n_items + 1

    def chunk(self, t):
        return t % self.nj

    def down_chunk(self, t):
        return (t + self.nj - 1) % self.nj

    def tile(self, t):
        return jnp.minimum(t, self.n_items - 1) // self.nj

    def down_tile(self, t):
        return jnp.maximum(t - 1, 0) // self.nj

    def tile_block(self, tile):
        return (tile // self.tiles_per_seq, tile % self.tiles_per_seq, 0)


def _run_step(steps, x_ref, g_ref, h_sc, act_bufs, carry_sc, o_ref, gate_fn, down_w_ref):
    t = pl.program_id(0)
    j = steps.chunk(t)

    @pl.when(t == 0)
    def _():
        for ref in (*act_bufs, carry_sc):
            ref[...] = jnp.zeros(ref.shape, ref.dtype)

    @pl.when((j == 0) & (t < steps.n_items))
    def _():
        h_sc[...] = _rms_norm(x_ref[...], g_ref[...]).astype(h_sc.dtype)

    seq_start = steps.tile(t) % steps.tiles_per_seq == 0
    first_down = (steps.down_chunk(t) == 0) | (t == 0)

    def body(act_out, act_in):
        carry_in = jnp.where(seq_start, 0.0, carry_sc[j])
        act, carry_out = gate_fn(h_sc[...], carry_in)
        carry_sc[j] = carry_out
        act_out[...] = act.astype(act_out.dtype)
        o_ref[...] = (jnp.where(first_down, x_ref[...], o_ref[...])
                      + _dot(act_in[...], down_w_ref[...]))

    for parity in (0, 1):
        pl.when(t % 2 == parity)(
            functools.partial(body, act_bufs[parity], act_bufs[1 - parity]))
    return t


def _ffn_kernel(x_ref, g_ref, wg_ref, wu_ref, cw_ref, cb_ref, wd_ref, fg_ref, o_ref,
                h_sc, act_a, act_b, carry_sc, *, steps, final_norm):
    tm = x_ref.shape[0]

    def gate_fn(h, carry_in):
        gate = _dot(h, wg_ref[...])
        up = _dot(h, wu_ref[...])
        a = _causal_conv3(gate, carry_in, cw_ref[...]) + cb_ref[...]
        return a * jax.nn.sigmoid(a) * up, gate[tm - SUBLANES:]

    t = _run_step(steps, x_ref, g_ref, h_sc, (act_a, act_b), carry_sc, o_ref, gate_fn, wd_ref)

    if final_norm:
        @pl.when((steps.down_chunk(t) == steps.nj - 1) & (t > 0))
        def _():
            o_ref[...] = _rms_norm(o_ref[...], fg_ref[...])


def _short_conv_kernel(x_ref, g_ref, wb_ref, wc_ref, wv_ref, cw_ref, wo_ref, o_ref,
                       h_sc, act_a, act_b, carry_sc, *, steps):
    tm = x_ref.shape[0]

    def gate_fn(h, carry_in):
        cv = _dot(h, wc_ref[...]) * _dot(h, wv_ref[...])
        u = _causal_conv3(cv, carry_in, cw_ref[...])
        return _dot(h, wb_ref[...]) * u, cv[tm - SUBLANES:]

    _run_step(steps, x_ref, g_ref, h_sc, (act_a, act_b), carry_sc, o_ref, gate_fn, wo_ref)


def _lagged_call(kernel_fn, name, x, steps, tm, tc, up_weights, per_chunk_rows, down_weight,
                 full_rows):
    d = x.shape[-1]
    nj = steps.nj
    x_spec = pl.BlockSpec((None, tm, d), lambda t: steps.tile_block(steps.tile(t)))
    row_spec = pl.BlockSpec((1, d), lambda t: (0, 0))
    in_specs = [x_spec, row_spec]
    operands = [x, full_rows[0]]
    for w, off in up_weights:
        in_specs.append(pl.BlockSpec((d, tc), lambda t, off=off: (0, steps.chunk(t) + off * nj)))
        operands.append(w)
    for r in per_chunk_rows:
        in_specs.append(pl.BlockSpec((r.shape[0], tc), lambda t: (0, steps.chunk(t))))
        operands.append(r)
    in_specs.append(pl.BlockSpec((tc, d), lambda t: (steps.down_chunk(t), 0)))
    operands.append(down_weight)
    for r in full_rows[1:]:
        in_specs.append(row_spec)
        operands.append(r)
    return pl.pallas_call(
        kernel_fn,
        out_shape=jax.ShapeDtypeStruct(x.shape, x.dtype),
        grid=(steps.n_steps,),
        in_specs=in_specs,
        out_specs=pl.BlockSpec((None, tm, d), lambda t: steps.tile_block(steps.down_tile(t))),
        scratch_shapes=[pltpu.VMEM((tm, d), down_weight.dtype)]
        + [pltpu.VMEM((tm, tc), down_weight.dtype)] * 2
        + [pltpu.VMEM((nj, SUBLANES, tc), jnp.float32)],
        compiler_params=_compiler_params(1),
        name=name,
    )(*operands)


def _ffn_block(x, norm_g, w_gate, w_up, conv_w, conv_b, w_down, final_g, final_norm):
    b, s, d = x.shape
    f = w_gate.shape[1]
    tm = _largest_tile(s, MAX_SEQ_TILE, 2 * SUBLANES)
    tf = _largest_tile(f, MAX_CHANNEL_TILE, LANES)
    steps = _Steps(f // tf, s // tm, b * (s // tm))
    row = lambda v: v.reshape(1, -1)
    return _lagged_call(
        functools.partial(_ffn_kernel, steps=steps, final_norm=final_norm), "conv_glu_ffn",
        x, steps, tm, tf, [(w_gate, 0), (w_up, 0)], [conv_w, row(conv_b)], w_down,
        [row(norm_g), row(final_g)])


def _short_conv_block(x, norm_g, w_in, conv_w, w_out):
    b, s, d = x.shape
    tm = _largest_tile(s, MAX_SEQ_TILE, 2 * SUBLANES)
    tc = _largest_tile(d, MAX_CHANNEL_TILE, LANES)
    steps = _Steps(d // tc, s // tm, b * (s // tm))
    return _lagged_call(
        functools.partial(_short_conv_kernel, steps=steps), "short_conv_mixer",
        x, steps, tm, tc, [(w_in, 0), (w_in, 1), (w_in, 2)], [conv_w], w_out,
        [norm_g.reshape(1, -1)])


def _pooling_kernel(x_ref, g_ref, wgrp_ref, scale_ref, o_ref, carry_sc):
    s = pl.program_id(1)
    tm = x_ref.shape[0]
    hist = carry_sc.shape[0]
    dg = wgrp_ref.shape[1]

    @pl.when(s == 0)
    def _():
        carry_sc[...] = jnp.zeros(carry_sc.shape, carry_sc.dtype)

    x = x_ref[...]
    h = _rms_norm(x, g_ref[...])
    prev = carry_sc[...]
    carry_sc[...] = h[tm - hist:]
    seen = (s * tm + 1 + lax.broadcasted_iota(jnp.int32, (tm, 1), 0)).astype(jnp.float32)
    for gi, win in enumerate(POOL_WINDOWS):
        sl = slice(gi * dg, (gi + 1) * dg)
        hg = h[:, sl]
        acc = jnp.concatenate([prev[:, sl], hg], axis=0)
        step = 1
        while step < win:
            acc = acc + pltpu.roll(acc, step, axis=0)
            step *= 2
        pooled = acc[hist:] * (1.0 / jnp.minimum(seen, float(win))) - hg
        mixed = _dot(pooled.astype(wgrp_ref.dtype), wgrp_ref[gi])
        o_ref[:, sl] = x[:, sl] + mixed * scale_ref[:, sl]


def _pooling_block(x, norm_g, w_group, scale):
    b, s, d = x.shape
    n_groups, dg, _ = w_group.shape
    assert n_groups == len(POOL_WINDOWS) and n_groups * dg == d
    hist = max(POOL_WINDOWS)
    tm = _largest_tile(s, MAX_SEQ_TILE, hist)
    return pl.pallas_call(
        _pooling_kernel,
        out_shape=jax.ShapeDtypeStruct(x.shape, x.dtype),
        grid=(b, s // tm),
        in_specs=[
            pl.BlockSpec((None, tm, d), lambda bi, si: (bi, si, 0)),
            pl.BlockSpec((1, d), lambda bi, si: (0, 0)),
            pl.BlockSpec((n_groups, dg, dg), lambda bi, si: (0, 0, 0)),
            pl.BlockSpec((1, d), lambda bi, si: (0, 0)),
        ],
        out_specs=pl.BlockSpec((None, tm, d), lambda bi, si: (bi, si, 0)),
        scratch_shapes=[pltpu.VMEM((hist, d), jnp.float32)],
        compiler_params=_compiler_params(2),
        name="pooling_mixer",
    )(x, norm_g.reshape(1, -1), w_group, scale.reshape(1, -1))


def _conv31_slab(ext_ref, sl, cw_ref, cb_ref, u_sc, col0):
    taps = cw_ref.shape[0]
    hist = ext_ref.shape[1] - u_sc.shape[0]
    lanes = slice(sl * LANES, (sl + 1) * LANES)
    for r0 in range(0, u_sc.shape[0], CONV_ROW_BLOCK):
        acc = jnp.broadcast_to(cb_ref[:, lanes], (CONV_ROW_BLOCK, LANES))
        for k in range(taps):
            rows = pl.ds(hist + r0 - (taps - 1 - k), CONV_ROW_BLOCK, stride=1)
            acc = acc + cw_ref[k:k + 1, lanes] * ext_ref[sl, rows, :]
        u_sc[r0:r0 + CONV_ROW_BLOCK, pl.ds(col0 + sl * LANES, LANES)] = acc


def _conformer_kernel(x_ref, xlag_ref, g_ref, wa_ref, wgt_ref, ba_ref, bgt_ref, cw_ref, cb_ref,
                      lng_ref, lnb_ref, w2_ref, b2_ref, o_ref,
                      h_sc, ext_sc, u_sc, act_sc, carry_sc, *, nc, tiles_per_seq, n_items):
    n = pl.program_id(0)
    c = n % nc
    tm = x_ref.shape[0]
    tc = wa_ref.shape[1]
    sub = _largest_tile(tc, MXU_COLUMNS, LANES)
    hist = carry_sc.shape[1]

    @pl.when(n == 0)
    def _():
        for ref in (act_sc, carry_sc):
            ref[...] = jnp.zeros(ref.shape, ref.dtype)

    @pl.when((c == 0) & (n < n_items))
    def _():
        h_sc[...] = _rms_norm(x_ref[...], g_ref[...]).astype(h_sc.dtype)

    @pl.when((c == 0) & (n >= nc))
    def _():
        u = u_sc[...]
        mu = jnp.mean(u, axis=-1, keepdims=True)
        uc = u - mu
        var = jnp.mean(uc * uc, axis=-1, keepdims=True)
        y = uc * lax.rsqrt(var + LN_EPS) * lng_ref[...] + lnb_ref[...]
        act_sc[...] = (y * jax.nn.sigmoid(y)).astype(act_sc.dtype)

    seq_start = (jnp.minimum(n, n_items - 1) // nc) % tiles_per_seq == 0
    h = h_sc[...]
    act = act_sc[...]
    carry_in = jnp.where(seq_start, 0.0, carry_sc[c])
    col0 = pl.multiple_of(c * tc, tc)
    for c0 in range(0, tc, sub):
        cols = slice(c0, c0 + sub)
        a = _dot(h, wa_ref[:, cols]) + ba_ref[:, cols]
        gt = _dot(h, wgt_ref[:, cols]) + bgt_ref[:, cols]
        u = a * jax.nn.sigmoid(gt)
        carry_sc[c, :, cols] = u[tm - hist:]
        for l0 in range(0, sub, LANES):
            sl = (c0 + l0) // LANES
            ext_sc[sl, :hist] = carry_in[:, c0 + l0:c0 + l0 + LANES]
            ext_sc[sl, hist:] = u[:, l0:l0 + LANES]
            _conv31_slab(ext_sc, sl, cw_ref, cb_ref, u_sc, col0)
    for c0 in range(0, tc, sub):
        cols = slice(c0, c0 + sub)
        o_ref[:, cols] = xlag_ref[:, cols] + _dot(act, w2_ref[:, cols]) + b2_ref[:, cols]


def _conformer_block(x, norm_g, w_pw1, b_pw1, conv_w, conv_b, ln_g, ln_b, w_pw2, b_pw2):
    b, s, d = x.shape
    taps = conv_w.shape[0]
    hist = -(-(taps - 1) // SUBLANES) * SUBLANES
    tm = _largest_tile(s, MAX_SEQ_TILE, CONV_ROW_BLOCK)
    tc = _largest_tile(d, MAX_CHANNEL_TILE, LANES)
    nc = d // tc
    tiles_per_seq = s // tm
    n_items = b * tiles_per_seq * nc
    assert tm % CONV_ROW_BLOCK == 0 and tm >= hist and tc % LANES == 0
    row = lambda v: v.reshape(1, -1)

    def in_tile(n):
        tile = jnp.minimum(n, n_items - 1) // nc
        return tile // tiles_per_seq, tile % tiles_per_seq

    def out_block(n):
        item = jnp.maximum(n - nc, 0)
        tile = item // nc
        return tile // tiles_per_seq, tile % tiles_per_seq, item % nc

    chunk = lambda n: n % nc
    return pl.pallas_call(
        functools.partial(_conformer_kernel, nc=nc, tiles_per_seq=tiles_per_seq,
                          n_items=n_items),
        out_shape=jax.ShapeDtypeStruct(x.shape, x.dtype),
        grid=(n_items + nc,),
        in_specs=[
            pl.BlockSpec((None, tm, d), lambda n: (*in_tile(n), 0)),
            pl.BlockSpec((None, tm, tc), out_block),
            pl.BlockSpec((1, d), lambda n: (0, 0)),
            pl.BlockSpec((d, tc), lambda n: (0, chunk(n))),
            pl.BlockSpec((d, tc), lambda n: (0, chunk(n) + nc)),
            pl.BlockSpec((1, tc), lambda n: (0, chunk(n))),
            pl.BlockSpec((1, tc), lambda n: (0, chunk(n) + nc)),
            pl.BlockSpec((taps, tc), lambda n: (0, chunk(n))),
            pl.BlockSpec((1, tc), lambda n: (0, chunk(n))),
            pl.BlockSpec((1, d), lambda n: (0, 0)),
            pl.BlockSpec((1, d), lambda n: (0, 0)),
            pl.BlockSpec((d, tc), lambda n: (0, chunk(n))),
            pl.BlockSpec((1, tc), lambda n: (0, chunk(n))),
        ],
        out_specs=pl.BlockSpec((None, tm, tc), out_block),
        scratch_shapes=[
            pltpu.VMEM((tm, d), w_pw1.dtype),
            pltpu.VMEM((tc // LANES, tm + hist, LANES), jnp.float32),
            pltpu.VMEM((tm, d), jnp.float32),
            pltpu.VMEM((tm, d), w_pw2.dtype),
            pltpu.VMEM((nc, hist, tc), jnp.float32),
        ],
        compiler_params=_compiler_params(1),
        name="conformer_mixer",
    )(x, x, row(norm_g), w_pw1, w_pw1, row(b_pw1), row(b_pw1), conv_w, row(conv_b),
      row(ln_g), row(ln_b), w_pw2, row(b_pw2))


def kernel(x, mix_norm_g, ffn_norm_g, a_w_in, a_conv_w, a_w_out, b_w_group, b_scale, c_w_pw1, c_b_pw1, c_conv_w, c_conv_b, c_ln_g, c_ln_b, c_w_pw2, c_b_pw2, f_w_gate, f_w_up, f_conv_w, f_conv_b, f_w_down, final_norm_g):
    mxu = MXU_DTYPE
    depth = mix_norm_g.shape[0]
    for i in range(depth):
        kind, j = i % N_MIXERS, i // N_MIXERS
        if kind == 0:
            x = _short_conv_block(x, mix_norm_g[i], a_w_in[j].astype(mxu), a_conv_w[j],
                                  a_w_out[j].astype(mxu))
        elif kind == 1:
            x = _pooling_block(x, mix_norm_g[i], b_w_group[j].astype(mxu), b_scale[j])
        else:
            x = _conformer_block(x, mix_norm_g[i], c_w_pw1[j].astype(mxu), c_b_pw1[j],
                                 c_conv_w[j], c_conv_b[j], c_ln_g[j], c_ln_b[j],
                                 c_w_pw2[j].astype(mxu), c_b_pw2[j])
        x = _ffn_block(x, ffn_norm_g[i], f_w_gate[i].astype(mxu), f_w_up[i].astype(mxu),
                       f_conv_w[i], f_conv_b[i], f_w_down[i].astype(mxu), final_norm_g,
                       final_norm=(i == depth - 1))
    return x
```

```python
import functools

import jax
import jax.numpy as jnp
from jax import lax
from jax.experimental import pallas as pl
from jax.experimental.pallas import tpu as pltpu

N_MIXERS = 3
POOL_WINDOWS = (2, 4, 8, 16)
RMS_EPS = 1e-5
LN_EPS = 1e-5

MXU_DTYPE = jnp.bfloat16
SUBLANES = 8
LANES = 128
MXU_COLUMNS = 256
MAX_SEQ_TILE = 512
MAX_CHANNEL_TILE = 512
CONV_ROW_BLOCK = 64
VMEM_LIMIT_BYTES = 56 * 1024 * 1024


def _largest_tile(n, cap, multiple):
    if n <= cap:
        return n
    t = cap - cap % multiple
    while t >= multiple:
        if n % t == 0:
            return t
        t -= multiple
    raise ValueError(f"no tile of {n} that is a multiple of {multiple} and <= {cap}")


def _compiler_params(n_grid_dims):
    return pltpu.CompilerParams(
        dimension_semantics=("arbitrary",) * n_grid_dims,
        vmem_limit_bytes=VMEM_LIMIT_BYTES,
    )


def _chunk_major(w, tc):
    d, n = w.shape
    return w.reshape(d, n // tc, tc).transpose(1, 0, 2)


def _rms_norm(x, g):
    return x * lax.rsqrt(jnp.mean(x * x, axis=-1, keepdims=True) + RMS_EPS) * g


def _dot(a, b):
    return jnp.dot(a, b, preferred_element_type=jnp.float32)


def _delay_rows(cur, prev_tail, k):
    rolled = pltpu.roll(cur, k, axis=0)
    tail = pltpu.roll(prev_tail, k, axis=0)
    rows = lax.broadcasted_iota(jnp.int32, prev_tail.shape, 0)
    head = jnp.where(rows < k, tail, rolled[:SUBLANES])
    return jnp.concatenate([head, rolled[SUBLANES:]], axis=0)


def _causal_conv3(cur, prev_tail, w):
    return (w[2:3] * cur + w[1:2] * _delay_rows(cur, prev_tail, 1)
            + w[0:1] * _delay_rows(cur, prev_tail, 2))


class _Steps:
    def __init__(self, nj, tiles_per_seq, n_tiles):
        assert nj >= 2, "a tile's input block must still be resident at its first down step"
        self.nj, self.tiles_per_seq, self.n_tiles = nj, tiles_per_seq, n_tiles
        self.n_items = n_tiles * nj
        self.n_steps = self.n_items + 1

    def chunk(self, t):
        return t % self.nj

    def down_chunk(self, t):
        return (t + self.nj - 1) % self.nj

    def tile(self, t):
        return jnp.minimum(t, self.n_items - 1) // self.nj

    def down_tile(self, t):
        return jnp.maximum(t - 1, 0) // self.nj

    def tile_block(self, tile):
        return (tile // self.tiles_per_seq, tile % self.tiles_per_seq, 0)


def _run_step(steps, x_ref, g_ref, h_sc, act_bufs, carry_sc, o_ref, gate_fn, down_w_ref):
    t = pl.program_id(0)
    j = steps.chunk(t)

    @pl.when(t == 0)
    def _():
        for ref in (*act_bufs, carry_sc):
            ref[...] = jnp.zeros(ref.shape, ref.dtype)

    @pl.when((j == 0) & (t < steps.n_items))
    def _():
        h_sc[...] = _rms_norm(x_ref[...], g_ref[...]).astype(h_sc.dtype)

    seq_start = steps.tile(t) % steps.tiles_per_seq == 0
    first_down = (steps.down_chunk(t) == 0) | (t == 0)

    def body(act_out, act_in):
        carry_in = jnp.where(seq_start, 0.0, carry_sc[j])
        act, carry_out = gate_fn(h_sc[...], carry_in)
        carry_sc[j] = carry_out
        act_out[...] = act.astype(act_out.dtype)
        o_ref[...] = (jnp.where(first_down, x_ref[...], o_ref[...])
                      + _dot(act_in[...], down_w_ref[...]))

    for parity in (0, 1):
        pl.when(t % 2 == parity)(
            functools.partial(body, act_bufs[parity], act_bufs[1 - parity]))
    return t


def _ffn_kernel(x_ref, g_ref, wg_ref, wu_ref, cw_ref, cb_ref, wd_ref, fg_ref, o_ref,
                h_sc, act_a, act_b, carry_sc, *, steps, final_norm):
    tm = x_ref.shape[0]

    def gate_fn(h, carry_in):
        gate = _dot(h, wg_ref[...])
        up = _dot(h, wu_ref[...])
        a = _causal_conv3(gate, carry_in, cw_ref[...]) + cb_ref[...]
        return a * jax.nn.sigmoid(a) * up, gate[tm - SUBLANES:]

    t = _run_step(steps, x_ref, g_ref, h_sc, (act_a, act_b), carry_sc, o_ref, gate_fn, wd_ref)

    if final_norm:
        @pl.when((steps.down_chunk(t) == steps.nj - 1) & (t > 0))
        def _():
            o_ref[...] = _rms_norm(o_ref[...], fg_ref[...])


def _short_conv_kernel(x_ref, g_ref, wb_ref, wc_ref, wv_ref, cw_ref, wo_ref, o_ref,
                       h_sc, act_a, act_b, carry_sc, *, steps):
    tm = x_ref.shape[0]

    def gate_fn(h, carry_in):
        cv = _dot(h, wc_ref[...]) * _dot(h, wv_ref[...])
        u = _causal_conv3(cv, carry_in, cw_ref[...])
        return _dot(h, wb_ref[...]) * u, cv[tm - SUBLANES:]

    _run_step(steps, x_ref, g_ref, h_sc, (act_a, act_b), carry_sc, o_ref, gate_fn, wo_ref)


def _lagged_call(kernel_fn, name, x, steps, tm, tc, up_weights, per_chunk_rows, down_weight,
                 full_rows):
    d = x.shape[-1]
    nj = steps.nj
    x_spec = pl.BlockSpec((None, tm, d), lambda t: steps.tile_block(steps.tile(t)))
    row_spec = pl.BlockSpec((1, d), lambda t: (0, 0))
    in_specs = [x_spec, row_spec]
    operands = [x, full_rows[0]]
    for w, off in up_weights:
        in_specs.append(pl.BlockSpec((None, d, tc),
                                     lambda t, off=off: (steps.chunk(t) + off * nj, 0, 0)))
        operands.append(w)
    for r in per_chunk_rows:
        in_specs.append(pl.BlockSpec((r.shape[0], tc), lambda t: (0, steps.chunk(t))))
        operands.append(r)
    in_specs.append(pl.BlockSpec((tc, d), lambda t: (steps.down_chunk(t), 0)))
    operands.append(down_weight)
    for r in full_rows[1:]:
        in_specs.append(row_spec)
        operands.append(r)
    return pl.pallas_call(
        kernel_fn,
        out_shape=jax.ShapeDtypeStruct(x.shape, x.dtype),
        grid=(steps.n_steps,),
        in_specs=in_specs,
        out_specs=pl.BlockSpec((None, tm, d), lambda t: steps.tile_block(steps.down_tile(t))),
        scratch_shapes=[pltpu.VMEM((tm, d), down_weight.dtype)]
        + [pltpu.VMEM((tm, tc), down_weight.dtype)] * 2
        + [pltpu.VMEM((nj, SUBLANES, tc), jnp.float32)],
        compiler_params=_compiler_params(1),
        name=name,
    )(*operands)


def _ffn_block(x, norm_g, w_gate, w_up, conv_w, conv_b, w_down, final_g, final_norm):
    b, s, d = x.shape
    f = w_gate.shape[1]
    tm = _largest_tile(s, MAX_SEQ_TILE, 2 * SUBLANES)
    tf = _largest_tile(f, MAX_CHANNEL_TILE, LANES)
    steps = _Steps(f // tf, s // tm, b * (s // tm))
    row = lambda v: v.reshape(1, -1)
    return _lagged_call(
        functools.partial(_ffn_kernel, steps=steps, final_norm=final_norm), "conv_glu_ffn",
        x, steps, tm, tf, [(_chunk_major(w_gate, tf), 0), (_chunk_major(w_up, tf), 0)],
        [conv_w, row(conv_b)], w_down,
        [row(norm_g), row(final_g)])


def _short_conv_block(x, norm_g, w_in, conv_w, w_out):
    b, s, d = x.shape
    tm = _largest_tile(s, MAX_SEQ_TILE, 2 * SUBLANES)
    tc = _largest_tile(d, MAX_CHANNEL_TILE, LANES)
    steps = _Steps(d // tc, s // tm, b * (s // tm))
    w_in = _chunk_major(w_in, tc)
    return _lagged_call(
        functools.partial(_short_conv_kernel, steps=steps), "short_conv_mixer",
        x, steps, tm, tc, [(w_in, 0), (w_in, 1), (w_in, 2)], [conv_w], w_out,
        [norm_g.reshape(1, -1)])


def _pooling_kernel(x_ref, g_ref, wgrp_ref, scale_ref, o_ref, carry_sc):
    s = pl.program_id(1)
    tm = x_ref.shape[0]
    hist = carry_sc.shape[0]
    dg = wgrp_ref.shape[1]

    @pl.when(s == 0)
    def _():
        carry_sc[...] = jnp.zeros(carry_sc.shape, carry_sc.dtype)

    x = x_ref[...]
    h = _rms_norm(x, g_ref[...])
    prev = carry_sc[...]
    carry_sc[...] = h[tm - hist:]
    seen = (s * tm + 1 + lax.broadcasted_iota(jnp.int32, (tm, 1), 0)).astype(jnp.float32)
    for gi, win in enumerate(POOL_WINDOWS):
        sl = slice(gi * dg, (gi + 1) * dg)
        hg = h[:, sl]
        acc = jnp.concatenate([prev[:, sl], hg], axis=0)
        step = 1
        while step < win:
            acc = acc + pltpu.roll(acc, step, axis=0)
            step *= 2
        pooled = acc[hist:] * (1.0 / jnp.minimum(seen, float(win))) - hg
        mixed = _dot(pooled.astype(wgrp_ref.dtype), wgrp_ref[gi])
        o_ref[:, sl] = x[:, sl] + mixed * scale_ref[:, sl]


def _pooling_block(x, norm_g, w_group, scale):
    b, s, d = x.shape
    n_groups, dg, _ = w_group.shape
    assert n_groups == len(POOL_WINDOWS) and n_groups * dg == d
    hist = max(POOL_WINDOWS)
    tm = _largest_tile(s, MAX_SEQ_TILE, hist)
    return pl.pallas_call(
        _pooling_kernel,
        out_shape=jax.ShapeDtypeStruct(x.shape, x.dtype),
        grid=(b, s // tm),
        in_specs=[
            pl.BlockSpec((None, tm, d), lambda bi, si: (bi, si, 0)),
            pl.BlockSpec((1, d), lambda bi, si: (0, 0)),
            pl.BlockSpec((n_groups, dg, dg), lambda bi, si: (0, 0, 0)),
            pl.BlockSpec((1, d), lambda bi, si: (0, 0)),
        ],
        out_specs=pl.BlockSpec((None, tm, d), lambda bi, si: (bi, si, 0)),
        scratch_shapes=[pltpu.VMEM((hist, d), jnp.float32)],
        compiler_params=_compiler_params(2),
        name="pooling_mixer",
    )(x, norm_g.reshape(1, -1), w_group, scale.reshape(1, -1))


def _conv31_slab(ext_ref, sl, cw_ref, cb_ref, u_sc, col0):
    taps = cw_ref.shape[0]
    hist = ext_ref.shape[1] - u_sc.shape[0]
    lanes = slice(sl * LANES, (sl + 1) * LANES)
    for r0 in range(0, u_sc.shape[0], CONV_ROW_BLOCK):
        acc = jnp.broadcast_to(cb_ref[:, lanes], (CONV_ROW_BLOCK, LANES))
        for k in range(taps):
            rows = pl.ds(hist + r0 - (taps - 1 - k), CONV_ROW_BLOCK, stride=1)
            acc = acc + cw_ref[k:k + 1, lanes] * ext_ref[sl, rows, :]
        u_sc[r0:r0 + CONV_ROW_BLOCK, pl.ds(col0 + sl * LANES, LANES)] = acc


def _conformer_kernel(x_ref, xlag_ref, g_ref, wa_ref, wgt_ref, ba_ref, bgt_ref, cw_ref, cb_ref,
                      lng_ref, lnb_ref, w2_ref, b2_ref, o_ref,
                      h_sc, ext_sc, u_sc, act_sc, carry_sc, *, nc, tiles_per_seq, n_items):
    n = pl.program_id(0)
    c = n % nc
    tm = x_ref.shape[0]
    tc = wa_ref.shape[1]
    sub = _largest_tile(tc, MXU_COLUMNS, LANES)
    hist = carry_sc.shape[1]

    @pl.when(n == 0)
    def _():
        for ref in (act_sc, carry_sc):
            ref[...] = jnp.zeros(ref.shape, ref.dtype)

    @pl.when((c == 0) & (n < n_items))
    def _():
        h_sc[...] = _rms_norm(x_ref[...], g_ref[...]).astype(h_sc.dtype)

    @pl.when((c == 0) & (n >= nc))
    def _():
        u = u_sc[...]
        mu = jnp.mean(u, axis=-1, keepdims=True)
        uc = u - mu
        var = jnp.mean(uc * uc, axis=-1, keepdims=True)
        y = uc * lax.rsqrt(var + LN_EPS) * lng_ref[...] + lnb_ref[...]
        act_sc[...] = (y * jax.nn.sigmoid(y)).astype(act_sc.dtype)

    seq_start = (jnp.minimum(n, n_items - 1) // nc) % tiles_per_seq == 0
    h = h_sc[...]
    act = act_sc[...]
    carry_in = jnp.where(seq_start, 0.0, carry_sc[c])
    col0 = pl.multiple_of(c * tc, tc)
    for c0 in range(0, tc, sub):
        cols = slice(c0, c0 + sub)
        a = _dot(h, wa_ref[:, cols]) + ba_ref[:, cols]
        gt = _dot(h, wgt_ref[:, cols]) + bgt_ref[:, cols]
        u = a * jax.nn.sigmoid(gt)
        carry_sc[c, :, cols] = u[tm - hist:]
        for l0 in range(0, sub, LANES):
            sl = (c0 + l0) // LANES
            ext_sc[sl, :hist] = carry_in[:, c0 + l0:c0 + l0 + LANES]
            ext_sc[sl, hist:] = u[:, l0:l0 + LANES]
            _conv31_slab(ext_sc, sl, cw_ref, cb_ref, u_sc, col0)
    for c0 in range(0, tc, sub):
        cols = slice(c0, c0 + sub)
        o_ref[:, cols] = xlag_ref[:, cols] + _dot(act, w2_ref[:, cols]) + b2_ref[:, cols]


def _conformer_block(x, norm_g, w_pw1, b_pw1, conv_w, conv_b, ln_g, ln_b, w_pw2, b_pw2):
    b, s, d = x.shape
    taps = conv_w.shape[0]
    hist = -(-(taps - 1) // SUBLANES) * SUBLANES
    tm = _largest_tile(s, MAX_SEQ_TILE, CONV_ROW_BLOCK)
    tc = _largest_tile(d, MAX_CHANNEL_TILE, LANES)
    nc = d // tc
    tiles_per_seq = s // tm
    n_items = b * tiles_per_seq * nc
    assert tm % CONV_ROW_BLOCK == 0 and tm >= hist and tc % LANES == 0
    row = lambda v: v.reshape(1, -1)

    def in_tile(n):
        tile = jnp.minimum(n, n_items - 1) // nc
        return tile // tiles_per_seq, tile % tiles_per_seq

    def out_block(n):
        item = jnp.maximum(n - nc, 0)
        tile = item // nc
        return tile // tiles_per_seq, tile % tiles_per_seq, item % nc

    chunk = lambda n: n % nc
    w_pw1, w_pw2 = _chunk_major(w_pw1, tc), _chunk_major(w_pw2, tc)
    return pl.pallas_call(
        functools.partial(_conformer_kernel, nc=nc, tiles_per_seq=tiles_per_seq,
                          n_items=n_items),
        out_shape=jax.ShapeDtypeStruct(x.shape, x.dtype),
        grid=(n_items + nc,),
        in_specs=[
            pl.BlockSpec((None, tm, d), lambda n: (*in_tile(n), 0)),
            pl.BlockSpec((None, tm, tc), out_block),
            pl.BlockSpec((1, d), lambda n: (0, 0)),
            pl.BlockSpec((None, d, tc), lambda n: (chunk(n), 0, 0)),
            pl.BlockSpec((None, d, tc), lambda n: (chunk(n) + nc, 0, 0)),
            pl.BlockSpec((1, tc), lambda n: (0, chunk(n))),
            pl.BlockSpec((1, tc), lambda n: (0, chunk(n) + nc)),
            pl.BlockSpec((taps, tc), lambda n: (0, chunk(n))),
            pl.BlockSpec((1, tc), lambda n: (0, chunk(n))),
            pl.BlockSpec((1, d), lambda n: (0, 0)),
            pl.BlockSpec((1, d), lambda n: (0, 0)),
            pl.BlockSpec((None, d, tc), lambda n: (chunk(n), 0, 0)),
            pl.BlockSpec((1, tc), lambda n: (0, chunk(n))),
        ],
        out_specs=pl.BlockSpec((None, tm, tc), out_block),
        scratch_shapes=[
            pltpu.VMEM((tm, d), w_pw1.dtype),
            pltpu.VMEM((tc // LANES, tm + hist, LANES), jnp.float32),
            pltpu.VMEM((tm, d), jnp.float32),
            pltpu.VMEM((tm, d), w_pw2.dtype),
            pltpu.VMEM((nc, hist, tc), jnp.float32),
        ],
        compiler_params=_compiler_params(1),
        name="conformer_mixer",
    )(x, x, row(norm_g), w_pw1, w_pw1, row(b_pw1), row(b_pw1), conv_w, row(conv_b),
      row(ln_g), row(ln_b), w_pw2, row(b_pw2))


def kernel(x, mix_norm_g, ffn_norm_g, a_w_in, a_conv_w, a_w_out, b_w_group, b_scale, c_w_pw1, c_b_pw1, c_conv_w, c_conv_b, c_ln_g, c_ln_b, c_w_pw2, c_b_pw2, f_w_gate, f_w_up, f_conv_w, f_conv_b, f_w_down, final_norm_g):
    mxu = MXU_DTYPE
    depth = mix_norm_g.shape[0]
    for i in range(depth):
        kind, j = i % N_MIXERS, i // N_MIXERS
        if kind == 0:
            x = _short_conv_block(x, mix_norm_g[i], a_w_in[j].astype(mxu), a_conv_w[j],
                                  a_w_out[j].astype(mxu))
        elif kind == 1:
            x = _pooling_block(x, mix_norm_g[i], b_w_group[j].astype(mxu), b_scale[j])
        else:
            x = _conformer_block(x, mix_norm_g[i], c_w_pw1[j].astype(mxu), c_b_pw1[j],
                                 c_conv_w[j], c_conv_b[j], c_ln_g[j], c_ln_b[j],
                                 c_w_pw2[j].astype(mxu), c_b_pw2[j])
        x = _ffn_block(x, ffn_norm_g[i], f_w_gate[i].astype(mxu), f_w_up[i].astype(mxu),
                       f_conv_w[i], f_conv_b[i], f_w_down[i].astype(mxu), final_norm_g,
                       final_norm=(i == depth - 1))
    return x
```

```python
import functools

import jax
import jax.numpy as jnp
from jax import lax
from jax.experimental import pallas as pl
from jax.experimental.pallas import tpu as pltpu

N_MIXERS = 3
POOL_WINDOWS = (2, 4, 8, 16)
RMS_EPS = 1e-5
LN_EPS = 1e-5

MXU_DTYPE = jnp.bfloat16
SUBLANES = 8
LANES = 128
MXU_COLUMNS = 256
MAX_SEQ_TILE = 512
MAX_FFN_SEQ_TILE = 1024
MAX_CHANNEL_TILE = 512
CONV_ROW_BLOCK = 64
V7X_VMEM_BYTES = 64 * 1024 * 1024
VMEM_LIMIT_BYTES = V7X_VMEM_BYTES - 8 * 1024 * 1024
FFN_VMEM_LIMIT_BYTES = V7X_VMEM_BYTES - 4 * 1024 * 1024


def _largest_tile(n, cap, multiple):
    if n <= cap:
        return n
    t = cap - cap % multiple
    while t >= multiple:
        if n % t == 0:
            return t
        t -= multiple
    raise ValueError(f"no tile of {n} that is a multiple of {multiple} and <= {cap}")


def _compiler_params(n_grid_dims, vmem_limit_bytes=VMEM_LIMIT_BYTES):
    return pltpu.CompilerParams(
        dimension_semantics=("arbitrary",) * n_grid_dims,
        vmem_limit_bytes=vmem_limit_bytes,
    )


def _rms_norm(x, g):
    return x * lax.rsqrt(jnp.mean(x * x, axis=-1, keepdims=True) + RMS_EPS) * g


def _dot(a, b):
    return jnp.dot(a, b, preferred_element_type=jnp.float32)


def _delay_rows(cur, prev_tail, k):
    rolled = pltpu.roll(cur, k, axis=0)
    tail = pltpu.roll(prev_tail, k, axis=0)
    rows = lax.broadcasted_iota(jnp.int32, prev_tail.shape, 0)
    head = jnp.where(rows < k, tail, rolled[:SUBLANES])
    return jnp.concatenate([head, rolled[SUBLANES:]], axis=0)


def _causal_conv3(cur, prev_tail, w):
    return (w[2:3] * cur + w[1:2] * _delay_rows(cur, prev_tail, 1)
            + w[0:1] * _delay_rows(cur, prev_tail, 2))


class _Steps:
    def __init__(self, nj, tiles_per_seq, n_tiles):
        assert nj >= 2, "a tile's input block must still be resident at its first down step"
        self.nj, self.tiles_per_seq, self.n_tiles = nj, tiles_per_seq, n_tiles
        self.n_items = n_tiles * nj
        self.n_steps = self.n_items + 1

    def chunk(self, t):
        return t % self.nj

    def down_chunk(self, t):
        return (t + self.nj - 1) % self.nj

    def tile(self, t):
        return jnp.minimum(t, self.n_items - 1) // self.nj

    def down_tile(self, t):
        return jnp.maximum(t - 1, 0) // self.nj

    def tile_block(self, tile):
        return (tile // self.tiles_per_seq, tile % self.tiles_per_seq, 0)


def _run_step(steps, x_ref, g_ref, h_sc, act_bufs, carry_sc, o_ref, gate_fn, down_w_ref):
    t = pl.program_id(0)
    j = steps.chunk(t)

    @pl.when(t == 0)
    def _():
        for ref in (*act_bufs, carry_sc):
            ref[...] = jnp.zeros(ref.shape, ref.dtype)

    @pl.when((j == 0) & (t < steps.n_items))
    def _():
        h_sc[...] = _rms_norm(x_ref[...], g_ref[...]).astype(h_sc.dtype)

    seq_start = steps.tile(t) % steps.tiles_per_seq == 0
    first_down = (steps.down_chunk(t) == 0) | (t == 0)

    def body(act_out, act_in):
        carry_in = jnp.where(seq_start, 0.0, carry_sc[j])
        act, carry_out = gate_fn(h_sc[...], carry_in)
        carry_sc[j] = carry_out
        act_out[...] = act.astype(act_out.dtype)
        o_ref[...] = (jnp.where(first_down, x_ref[...], o_ref[...])
                      + _dot(act_in[...], down_w_ref[...]))

    for parity in (0, 1):
        pl.when(t % 2 == parity)(
            functools.partial(body, act_bufs[parity], act_bufs[1 - parity]))
    return t


def _ffn_kernel(x_ref, g_ref, wg_ref, wu_ref, cw_ref, cb_ref, wd_ref, fg_ref, o_ref,
                h_sc, act_a, act_b, carry_sc, *, steps, final_norm):
    tm = x_ref.shape[0]

    def gate_fn(h, carry_in):
        gate = _dot(h, wg_ref[...])
        up = _dot(h, wu_ref[...])
        a = _causal_conv3(gate, carry_in, cw_ref[...]) + cb_ref[...]
        return a * jax.nn.sigmoid(a) * up, gate[tm - SUBLANES:]

    t = _run_step(steps, x_ref, g_ref, h_sc, (act_a, act_b), carry_sc, o_ref, gate_fn, wd_ref)

    if final_norm:
        @pl.when((steps.down_chunk(t) == steps.nj - 1) & (t > 0))
        def _():
            o_ref[...] = _rms_norm(o_ref[...], fg_ref[...])


def _short_conv_kernel(x_ref, g_ref, wb_ref, wc_ref, wv_ref, cw_ref, wo_ref, o_ref,
                       h_sc, act_a, act_b, carry_sc, *, steps):
    tm = x_ref.shape[0]

    def gate_fn(h, carry_in):
        cv = _dot(h, wc_ref[...]) * _dot(h, wv_ref[...])
        u = _causal_conv3(cv, carry_in, cw_ref[...])
        return _dot(h, wb_ref[...]) * u, cv[tm - SUBLANES:]

    _run_step(steps, x_ref, g_ref, h_sc, (act_a, act_b), carry_sc, o_ref, gate_fn, wo_ref)


def _lagged_call(kernel_fn, name, x, steps, tm, tc, up_weights, per_chunk_rows, down_weight,
                 full_rows, vmem_limit_bytes=VMEM_LIMIT_BYTES):
    d = x.shape[-1]
    nj = steps.nj
    x_spec = pl.BlockSpec((None, tm, d), lambda t: steps.tile_block(steps.tile(t)))
    row_spec = pl.BlockSpec((1, d), lambda t: (0, 0))
    in_specs = [x_spec, row_spec]
    operands = [x, full_rows[0]]
    for w, off in up_weights:
        in_specs.append(pl.BlockSpec((d, tc), lambda t, off=off: (0, steps.chunk(t) + off * nj)))
        operands.append(w)
    for r in per_chunk_rows:
        in_specs.append(pl.BlockSpec((r.shape[0], tc), lambda t: (0, steps.chunk(t))))
        operands.append(r)
    in_specs.append(pl.BlockSpec((tc, d), lambda t: (steps.down_chunk(t), 0)))
    operands.append(down_weight)
    for r in full_rows[1:]:
        in_specs.append(row_spec)
        operands.append(r)
    return pl.pallas_call(
        kernel_fn,
        out_shape=jax.ShapeDtypeStruct(x.shape, x.dtype),
        grid=(steps.n_steps,),
        in_specs=in_specs,
        out_specs=pl.BlockSpec((None, tm, d), lambda t: steps.tile_block(steps.down_tile(t))),
        scratch_shapes=[pltpu.VMEM((tm, d), down_weight.dtype)]
        + [pltpu.VMEM((tm, tc), down_weight.dtype)] * 2
        + [pltpu.VMEM((nj, SUBLANES, tc), jnp.float32)],
        compiler_params=_compiler_params(1, vmem_limit_bytes),
        name=name,
    )(*operands)


def _ffn_block(x, norm_g, w_gate, w_up, conv_w, conv_b, w_down, final_g, final_norm):
    b, s, d = x.shape
    f = w_gate.shape[1]
    tm = _largest_tile(s, MAX_FFN_SEQ_TILE, 2 * SUBLANES)
    tf = _largest_tile(f, MAX_CHANNEL_TILE, LANES)
    steps = _Steps(f // tf, s // tm, b * (s // tm))
    row = lambda v: v.reshape(1, -1)
    return _lagged_call(
        functools.partial(_ffn_kernel, steps=steps, final_norm=final_norm), "conv_glu_ffn",
        x, steps, tm, tf, [(w_gate, 0), (w_up, 0)], [conv_w, row(conv_b)], w_down,
        [row(norm_g), row(final_g)], vmem_limit_bytes=FFN_VMEM_LIMIT_BYTES)


def _short_conv_block(x, norm_g, w_in, conv_w, w_out):
    b, s, d = x.shape
    tm = _largest_tile(s, MAX_SEQ_TILE, 2 * SUBLANES)
    tc = _largest_tile(d, MAX_CHANNEL_TILE, LANES)
    steps = _Steps(d // tc, s // tm, b * (s // tm))
    return _lagged_call(
        functools.partial(_short_conv_kernel, steps=steps), "short_conv_mixer",
        x, steps, tm, tc, [(w_in, 0), (w_in, 1), (w_in, 2)], [conv_w], w_out,
        [norm_g.reshape(1, -1)])


def _pooling_kernel(x_ref, g_ref, wgrp_ref, scale_ref, o_ref, carry_sc):
    s = pl.program_id(1)
    tm = x_ref.shape[0]
    hist = carry_sc.shape[0]
    dg = wgrp_ref.shape[1]

    @pl.when(s == 0)
    def _():
        carry_sc[...] = jnp.zeros(carry_sc.shape, carry_sc.dtype)

    x = x_ref[...]
    h = _rms_norm(x, g_ref[...])
    prev = carry_sc[...]
    carry_sc[...] = h[tm - hist:]
    seen = (s * tm + 1 + lax.broadcasted_iota(jnp.int32, (tm, 1), 0)).astype(jnp.float32)
    for gi, win in enumerate(POOL_WINDOWS):
        sl = slice(gi * dg, (gi + 1) * dg)
        hg = h[:, sl]
        acc = jnp.concatenate([prev[:, sl], hg], axis=0)
        step = 1
        while step < win:
            acc = acc + pltpu.roll(acc, step, axis=0)
            step *= 2
        pooled = acc[hist:] * (1.0 / jnp.minimum(seen, float(win))) - hg
        mixed = _dot(pooled.astype(wgrp_ref.dtype), wgrp_ref[gi])
        o_ref[:, sl] = x[:, sl] + mixed * scale_ref[:, sl]


def _pooling_block(x, norm_g, w_group, scale):
    b, s, d = x.shape
    n_groups, dg, _ = w_group.shape
    assert n_groups == len(POOL_WINDOWS) and n_groups * dg == d
    hist = max(POOL_WINDOWS)
    tm = _largest_tile(s, MAX_SEQ_TILE, hist)
    return pl.pallas_call(
        _pooling_kernel,
        out_shape=jax.ShapeDtypeStruct(x.shape, x.dtype),
        grid=(b, s // tm),
        in_specs=[
            pl.BlockSpec((None, tm, d), lambda bi, si: (bi, si, 0)),
            pl.BlockSpec((1, d), lambda bi, si: (0, 0)),
            pl.BlockSpec((n_groups, dg, dg), lambda bi, si: (0, 0, 0)),
            pl.BlockSpec((1, d), lambda bi, si: (0, 0)),
        ],
        out_specs=pl.BlockSpec((None, tm, d), lambda bi, si: (bi, si, 0)),
        scratch_shapes=[pltpu.VMEM((hist, d), jnp.float32)],
        compiler_params=_compiler_params(2),
        name="pooling_mixer",
    )(x, norm_g.reshape(1, -1), w_group, scale.reshape(1, -1))


def _conv31_slab(ext_ref, sl, cw_ref, cb_ref, u_sc, col0):
    taps = cw_ref.shape[0]
    hist = ext_ref.shape[1] - u_sc.shape[0]
    lanes = slice(sl * LANES, (sl + 1) * LANES)
    for r0 in range(0, u_sc.shape[0], CONV_ROW_BLOCK):
        acc = jnp.broadcast_to(cb_ref[:, lanes], (CONV_ROW_BLOCK, LANES))
        for k in range(taps):
            rows = pl.ds(hist + r0 - (taps - 1 - k), CONV_ROW_BLOCK, stride=1)
            acc = acc + cw_ref[k:k + 1, lanes] * ext_ref[sl, rows, :]
        u_sc[r0:r0 + CONV_ROW_BLOCK, pl.ds(col0 + sl * LANES, LANES)] = acc


def _conformer_kernel(x_ref, xlag_ref, g_ref, wa_ref, wgt_ref, ba_ref, bgt_ref, cw_ref, cb_ref,
                      lng_ref, lnb_ref, w2_ref, b2_ref, o_ref,
                      h_sc, ext_sc, u_sc, act_sc, carry_sc, *, nc, tiles_per_seq, n_items):
    n = pl.program_id(0)
    c = n % nc
    tm = x_ref.shape[0]
    tc = wa_ref.shape[1]
    sub = _largest_tile(tc, MXU_COLUMNS, LANES)
    hist = carry_sc.shape[1]

    @pl.when(n == 0)
    def _():
        for ref in (act_sc, carry_sc):
            ref[...] = jnp.zeros(ref.shape, ref.dtype)

    @pl.when((c == 0) & (n < n_items))
    def _():
        h_sc[...] = _rms_norm(x_ref[...], g_ref[...]).astype(h_sc.dtype)

    @pl.when((c == 0) & (n >= nc))
    def _():
        u = u_sc[...]
        mu = jnp.mean(u, axis=-1, keepdims=True)
        uc = u - mu
        var = jnp.mean(uc * uc, axis=-1, keepdims=True)
        y = uc * lax.rsqrt(var + LN_EPS) * lng_ref[...] + lnb_ref[...]
        act_sc[...] = (y * jax.nn.sigmoid(y)).astype(act_sc.dtype)

    seq_start = (jnp.minimum(n, n_items - 1) // nc) % tiles_per_seq == 0
    h = h_sc[...]
    act = act_sc[...]
    carry_in = jnp.where(seq_start, 0.0, carry_sc[c])
    col0 = pl.multiple_of(c * tc, tc)
    for c0 in range(0, tc, sub):
        cols = slice(c0, c0 + sub)
        a = _dot(h, wa_ref[:, cols]) + ba_ref[:, cols]
        gt = _dot(h, wgt_ref[:, cols]) + bgt_ref[:, cols]
        u = a * jax.nn.sigmoid(gt)
        carry_sc[c, :, cols] = u[tm - hist:]
        for l0 in range(0, sub, LANES):
            sl = (c0 + l0) // LANES
            ext_sc[sl, :hist] = carry_in[:, c0 + l0:c0 + l0 + LANES]
            ext_sc[sl, hist:] = u[:, l0:l0 + LANES]
            _conv31_slab(ext_sc, sl, cw_ref, cb_ref, u_sc, col0)
    for c0 in range(0, tc, sub):
        cols = slice(c0, c0 + sub)
        o_ref[:, cols] = xlag_ref[:, cols] + _dot(act, w2_ref[:, cols]) + b2_ref[:, cols]


def _conformer_block(x, norm_g, w_pw1, b_pw1, conv_w, conv_b, ln_g, ln_b, w_pw2, b_pw2):
    b, s, d = x.shape
    taps = conv_w.shape[0]
    hist = -(-(taps - 1) // SUBLANES) * SUBLANES
    tm = _largest_tile(s, MAX_SEQ_TILE, CONV_ROW_BLOCK)
    tc = _largest_tile(d, MAX_CHANNEL_TILE, LANES)
    nc = d // tc
    tiles_per_seq = s // tm
    n_items = b * tiles_per_seq * nc
    assert tm % CONV_ROW_BLOCK == 0 and tm >= hist and tc % LANES == 0
    row = lambda v: v.reshape(1, -1)

    def in_tile(n):
        tile = jnp.minimum(n, n_items - 1) // nc
        return tile // tiles_per_seq, tile % tiles_per_seq

    def out_block(n):
        item = jnp.maximum(n - nc, 0)
        tile = item // nc
        return tile // tiles_per_seq, tile % tiles_per_seq, item % nc

    chunk = lambda n: n % nc
    return pl.pallas_call(
        functools.partial(_conformer_kernel, nc=nc, tiles_per_seq=tiles_per_seq,
                          n_items=n_items),
        out_shape=jax.ShapeDtypeStruct(x.shape, x.dtype),
        grid=(n_items + nc,),
        in_specs=[
            pl.BlockSpec((None, tm, d), lambda n: (*in_tile(n), 0)),
            pl.BlockSpec((None, tm, tc), out_block),
            pl.BlockSpec((1, d), lambda n: (0, 0)),
            pl.BlockSpec((d, tc), lambda n: (0, chunk(n))),
            pl.BlockSpec((d, tc), lambda n: (0, chunk(n) + nc)),
            pl.BlockSpec((1, tc), lambda n: (0, chunk(n))),
            pl.BlockSpec((1, tc), lambda n: (0, chunk(n) + nc)),
            pl.BlockSpec((taps, tc), lambda n: (0, chunk(n))),
            pl.BlockSpec((1, tc), lambda n: (0, chunk(n))),
            pl.BlockSpec((1, d), lambda n: (0, 0)),
            pl.BlockSpec((1, d), lambda n: (0, 0)),
            pl.BlockSpec((d, tc), lambda n: (0, chunk(n))),
            pl.BlockSpec((1, tc), lambda n: (0, chunk(n))),
        ],
        out_specs=pl.BlockSpec((None, tm, tc), out_block),
        scratch_shapes=[
            pltpu.VMEM((tm, d), w_pw1.dtype),
            pltpu.VMEM((tc // LANES, tm + hist, LANES), jnp.float32),
            pltpu.VMEM((tm, d), jnp.float32),
            pltpu.VMEM((tm, d), w_pw2.dtype),
            pltpu.VMEM((nc, hist, tc), jnp.float32),
        ],
        compiler_params=_compiler_params(1),
        name="conformer_mixer",
    )(x, x, row(norm_g), w_pw1, w_pw1, row(b_pw1), row(b_pw1), conv_w, row(conv_b),
      row(ln_g), row(ln_b), w_pw2, row(b_pw2))


def kernel(x, mix_norm_g, ffn_norm_g, a_w_in, a_conv_w, a_w_out, b_w_group, b_scale, c_w_pw1, c_b_pw1, c_conv_w, c_conv_b, c_ln_g, c_ln_b, c_w_pw2, c_b_pw2, f_w_gate, f_w_up, f_conv_w, f_conv_b, f_w_down, final_norm_g):
    mxu = MXU_DTYPE
    depth = mix_norm_g.shape[0]
    for i in range(depth):
        kind, j = i % N_MIXERS, i // N_MIXERS
        if kind == 0:
            x = _short_conv_block(x, mix_norm_g[i], a_w_in[j].astype(mxu), a_conv_w[j],
                                  a_w_out[j].astype(mxu))
        elif kind == 1:
            x = _pooling_block(x, mix_norm_g[i], b_w_group[j].astype(mxu), b_scale[j])
        else:
            x = _conformer_block(x, mix_norm_g[i], c_w_pw1[j].astype(mxu), c_b_pw1[j],
                                 c_conv_w[j], c_conv_b[j], c_ln_g[j], c_ln_b[j],
                                 c_w_pw2[j].astype(mxu), c_b_pw2[j])
        x = _ffn_block(x, ffn_norm_g[i], f_w_gate[i].astype(mxu), f_w_up[i].astype(mxu),
                       f_conv_w[i], f_conv_b[i], f_w_down[i].astype(mxu), final_norm_g,
                       final_norm=(i == depth - 1))
    return x
```

```python
import functools

import jax
import jax.numpy as jnp
from jax import lax
from jax.experimental import pallas as pl
from jax.experimental.pallas import tpu as pltpu

N_MIXERS = 3
POOL_WINDOWS = (2, 4, 8, 16)
RMS_EPS = 1e-5
LN_EPS = 1e-5

MXU_DTYPE = jnp.bfloat16
SUBLANES = 8
LANES = 128
MXU_COLUMNS = 256
MAX_SEQ_TILE = 512
MAX_FFN_SEQ_TILE = 1024
MAX_CHANNEL_TILE = 512
CONV_ROW_BLOCK = 64
V7X_VMEM_BYTES = 64 * 1024 * 1024
VMEM_LIMIT_BYTES = V7X_VMEM_BYTES - 8 * 1024 * 1024
FFN_VMEM_LIMIT_BYTES = V7X_VMEM_BYTES - 4 * 1024 * 1024


def _largest_tile(n, cap, multiple):
    if n <= cap:
        return n
    t = cap - cap % multiple
    while t >= multiple:
        if n % t == 0:
            return t
        t -= multiple
    raise ValueError(f"no tile of {n} that is a multiple of {multiple} and <= {cap}")


def _compiler_params(n_grid_dims, vmem_limit_bytes=VMEM_LIMIT_BYTES):
    return pltpu.CompilerParams(
        dimension_semantics=("arbitrary",) * n_grid_dims,
        vmem_limit_bytes=vmem_limit_bytes,
    )


def _rms_norm(x, g):
    return x * lax.rsqrt(jnp.mean(x * x, axis=-1, keepdims=True) + RMS_EPS) * g


def _dot(a, b):
    return jnp.dot(a, b, preferred_element_type=jnp.float32)


def _delay_rows(cur, prev_tail, k):
    rolled = pltpu.roll(cur, k, axis=0)
    tail = pltpu.roll(prev_tail, k, axis=0)
    rows = lax.broadcasted_iota(jnp.int32, prev_tail.shape, 0)
    head = jnp.where(rows < k, tail, rolled[:SUBLANES])
    return jnp.concatenate([head, rolled[SUBLANES:]], axis=0)


def _causal_conv3(cur, prev_tail, w):
    return (w[2:3] * cur + w[1:2] * _delay_rows(cur, prev_tail, 1)
            + w[0:1] * _delay_rows(cur, prev_tail, 2))


class _Steps:
    def __init__(self, nj, tiles_per_seq, n_tiles):
        assert nj >= 2, "a tile's input block must still be resident at its first down step"
        self.nj, self.tiles_per_seq, self.n_tiles = nj, tiles_per_seq, n_tiles
        self.n_items = n_tiles * nj
        self.n_steps = self.n_items + 1

    def chunk(self, t):
        return t % self.nj

    def down_chunk(self, t):
        return (t + self.nj - 1) % self.nj

    def tile(self, t):
        return jnp.minimum(t, self.n_items - 1) // self.nj

    def down_tile(self, t):
        return jnp.maximum(t - 1, 0) // self.nj

    def tile_block(self, tile):
        return (tile // self.tiles_per_seq, tile % self.tiles_per_seq, 0)


def _run_step(steps, x_ref, g_ref, h_sc, act_bufs, carry_sc, o_ref, gate_fn, down_w_ref):
    t = pl.program_id(0)
    j = steps.chunk(t)

    @pl.when(t == 0)
    def _():
        for ref in (*act_bufs, carry_sc):
            ref[...] = jnp.zeros(ref.shape, ref.dtype)

    @pl.when((j == 0) & (t < steps.n_items))
    def _():
        h_sc[...] = _rms_norm(x_ref[...], g_ref[...]).astype(h_sc.dtype)

    seq_start = steps.tile(t) % steps.tiles_per_seq == 0
    first_down = (steps.down_chunk(t) == 0) | (t == 0)

    def body(act_out, act_in):
        carry_in = jnp.where(seq_start, 0.0, carry_sc[j])
        act, carry_out = gate_fn(h_sc[...], carry_in)
        carry_sc[j] = carry_out
        act_out[...] = act.astype(act_out.dtype)
        o_ref[...] = (jnp.where(first_down, x_ref[...], o_ref[...])
                      + _dot(act_in[...], down_w_ref[...]))

    for parity in (0, 1):
        pl.when(t % 2 == parity)(
            functools.partial(body, act_bufs[parity], act_bufs[1 - parity]))
    return t


def _ffn_kernel(x_ref, g_ref, wg_ref, wu_ref, cw_ref, cb_ref, wd_ref, fg_ref, o_ref,
                h_sc, act_a, act_b, carry_sc, *, steps, final_norm):
    tm = x_ref.shape[0]

    def gate_fn(h, carry_in):
        gate = _dot(h, wg_ref[...])
        up = _dot(h, wu_ref[...])
        a = _causal_conv3(gate, carry_in, cw_ref[...]) + cb_ref[...]
        return a * jax.nn.sigmoid(a) * up, gate[tm - SUBLANES:]

    t = _run_step(steps, x_ref, g_ref, h_sc, (act_a, act_b), carry_sc, o_ref, gate_fn, wd_ref)

    if final_norm:
        @pl.when((steps.down_chunk(t) == steps.nj - 1) & (t > 0))
        def _():
            o_ref[...] = _rms_norm(o_ref[...], fg_ref[...])


def _short_conv_kernel(x_ref, g_ref, wb_ref, wc_ref, wv_ref, cw_ref, wo_ref, o_ref,
                       h_sc, act_a, act_b, carry_sc, *, steps):
    tm = x_ref.shape[0]

    def gate_fn(h, carry_in):
        cv = _dot(h, wc_ref[...]) * _dot(h, wv_ref[...])
        u = _causal_conv3(cv, carry_in, cw_ref[...])
        return _dot(h, wb_ref[...]) * u, cv[tm - SUBLANES:]

    _run_step(steps, x_ref, g_ref, h_sc, (act_a, act_b), carry_sc, o_ref, gate_fn, wo_ref)


def _lagged_call(kernel_fn, name, x, steps, tm, tc, up_weights, per_chunk_rows, down_weight,
                 full_rows, vmem_limit_bytes=VMEM_LIMIT_BYTES):
    d = x.shape[-1]
    nj = steps.nj
    x_spec = pl.BlockSpec((None, tm, d), lambda t: steps.tile_block(steps.tile(t)))
    row_spec = pl.BlockSpec((1, d), lambda t: (0, 0))
    in_specs = [x_spec, row_spec]
    operands = [x, full_rows[0]]
    for w, off in up_weights:
        in_specs.append(pl.BlockSpec((d, tc), lambda t, off=off: (0, steps.chunk(t) + off * nj)))
        operands.append(w)
    for r in per_chunk_rows:
        in_specs.append(pl.BlockSpec((r.shape[0], tc), lambda t: (0, steps.chunk(t))))
        operands.append(r)
    in_specs.append(pl.BlockSpec((tc, d), lambda t: (steps.down_chunk(t), 0)))
    operands.append(down_weight)
    for r in full_rows[1:]:
        in_specs.append(row_spec)
        operands.append(r)
    return pl.pallas_call(
        kernel_fn,
        out_shape=jax.ShapeDtypeStruct(x.shape, x.dtype),
        grid=(steps.n_steps,),
        in_specs=in_specs,
        out_specs=pl.BlockSpec((None, tm, d), lambda t: steps.tile_block(steps.down_tile(t))),
        scratch_shapes=[pltpu.VMEM((tm, d), down_weight.dtype)]
        + [pltpu.VMEM((tm, tc), down_weight.dtype)] * 2
        + [pltpu.VMEM((nj, SUBLANES, tc), jnp.float32)],
        compiler_params=_compiler_params(1, vmem_limit_bytes),
        name=name,
    )(*operands)


def _ffn_block(x, norm_g, w_gate, w_up, conv_w, conv_b, w_down, final_g, final_norm):
    b, s, d = x.shape
    f = w_gate.shape[1]
    tm = _largest_tile(s, MAX_FFN_SEQ_TILE, 2 * SUBLANES)
    tf = _largest_tile(f, MAX_CHANNEL_TILE, LANES)
    steps = _Steps(f // tf, s // tm, b * (s // tm))
    row = lambda v: v.reshape(1, -1)
    return _lagged_call(
        functools.partial(_ffn_kernel, steps=steps, final_norm=final_norm), "conv_glu_ffn",
        x, steps, tm, tf, [(w_gate, 0), (w_up, 0)], [conv_w, row(conv_b)], w_down,
        [row(norm_g), row(final_g)], vmem_limit_bytes=FFN_VMEM_LIMIT_BYTES)


def _short_conv_block(x, norm_g, w_in, conv_w, w_out):
    b, s, d = x.shape
    tm = _largest_tile(s, MAX_FFN_SEQ_TILE, 2 * SUBLANES)
    tc = _largest_tile(d, MXU_COLUMNS, LANES)
    steps = _Steps(d // tc, s // tm, b * (s // tm))
    return _lagged_call(
        functools.partial(_short_conv_kernel, steps=steps), "short_conv_mixer",
        x, steps, tm, tc, [(w_in, 0), (w_in, 1), (w_in, 2)], [conv_w], w_out,
        [norm_g.reshape(1, -1)], vmem_limit_bytes=FFN_VMEM_LIMIT_BYTES)


def _pooling_kernel(x_ref, g_ref, wgrp_ref, scale_ref, o_ref, carry_sc):
    s = pl.program_id(1)
    tm = x_ref.shape[0]
    hist = carry_sc.shape[0]
    dg = wgrp_ref.shape[1]

    @pl.when(s == 0)
    def _():
        carry_sc[...] = jnp.zeros(carry_sc.shape, carry_sc.dtype)

    x = x_ref[...]
    h = _rms_norm(x, g_ref[...])
    prev = carry_sc[...]
    carry_sc[...] = h[tm - hist:]
    seen = (s * tm + 1 + lax.broadcasted_iota(jnp.int32, (tm, 1), 0)).astype(jnp.float32)
    for gi, win in enumerate(POOL_WINDOWS):
        sl = slice(gi * dg, (gi + 1) * dg)
        hg = h[:, sl]
        acc = jnp.concatenate([prev[:, sl], hg], axis=0)
        step = 1
        while step < win:
            acc = acc + pltpu.roll(acc, step, axis=0)
            step *= 2
        pooled = acc[hist:] * (1.0 / jnp.minimum(seen, float(win))) - hg
        mixed = _dot(pooled.astype(wgrp_ref.dtype), wgrp_ref[gi])
        o_ref[:, sl] = x[:, sl] + mixed * scale_ref[:, sl]


def _pooling_block(x, norm_g, w_group, scale):
    b, s, d = x.shape
    n_groups, dg, _ = w_group.shape
    assert n_groups == len(POOL_WINDOWS) and n_groups * dg == d
    hist = max(POOL_WINDOWS)
    tm = _largest_tile(s, MAX_SEQ_TILE, hist)
    return pl.pallas_call(
        _pooling_kernel,
        out_shape=jax.ShapeDtypeStruct(x.shape, x.dtype),
        grid=(b, s // tm),
        in_specs=[
            pl.BlockSpec((None, tm, d), lambda bi, si: (bi, si, 0)),
            pl.BlockSpec((1, d), lambda bi, si: (0, 0)),
            pl.BlockSpec((n_groups, dg, dg), lambda bi, si: (0, 0, 0)),
            pl.BlockSpec((1, d), lambda bi, si: (0, 0)),
        ],
        out_specs=pl.BlockSpec((None, tm, d), lambda bi, si: (bi, si, 0)),
        scratch_shapes=[pltpu.VMEM((hist, d), jnp.float32)],
        compiler_params=_compiler_params(2),
        name="pooling_mixer",
    )(x, norm_g.reshape(1, -1), w_group, scale.reshape(1, -1))


def _conv31_slab(ext_ref, sl, cw_ref, cb_ref, u_sc, col0):
    taps = cw_ref.shape[0]
    hist = ext_ref.shape[1] - u_sc.shape[0]
    lanes = slice(sl * LANES, (sl + 1) * LANES)
    for r0 in range(0, u_sc.shape[0], CONV_ROW_BLOCK):
        acc = jnp.broadcast_to(cb_ref[:, lanes], (CONV_ROW_BLOCK, LANES))
        for k in range(taps):
            rows = pl.ds(hist + r0 - (taps - 1 - k), CONV_ROW_BLOCK, stride=1)
            acc = acc + cw_ref[k:k + 1, lanes] * ext_ref[sl, rows, :]
        u_sc[r0:r0 + CONV_ROW_BLOCK, pl.ds(col0 + sl * LANES, LANES)] = acc


def _conformer_kernel(x_ref, xlag_ref, g_ref, wa_ref, wgt_ref, ba_ref, bgt_ref, cw_ref, cb_ref,
                      lng_ref, lnb_ref, w2_ref, b2_ref, o_ref,
                      h_sc, ext_sc, u_sc, act_sc, carry_sc, *, nc, tiles_per_seq, n_items):
    n = pl.program_id(0)
    c = n % nc
    tm = x_ref.shape[0]
    tc = wa_ref.shape[1]
    sub = _largest_tile(tc, MXU_COLUMNS, LANES)
    hist = carry_sc.shape[1]

    @pl.when(n == 0)
    def _():
        for ref in (act_sc, carry_sc):
            ref[...] = jnp.zeros(ref.shape, ref.dtype)

    @pl.when((c == 0) & (n < n_items))
    def _():
        h_sc[...] = _rms_norm(x_ref[...], g_ref[...]).astype(h_sc.dtype)

    @pl.when((c == 0) & (n >= nc))
    def _():
        u = u_sc[...]
        mu = jnp.mean(u, axis=-1, keepdims=True)
        uc = u - mu
        var = jnp.mean(uc * uc, axis=-1, keepdims=True)
        y = uc * lax.rsqrt(var + LN_EPS) * lng_ref[...] + lnb_ref[...]
        act_sc[...] = (y * jax.nn.sigmoid(y)).astype(act_sc.dtype)

    seq_start = (jnp.minimum(n, n_items - 1) // nc) % tiles_per_seq == 0
    h = h_sc[...]
    act = act_sc[...]
    carry_in = jnp.where(seq_start, 0.0, carry_sc[c])
    col0 = pl.multiple_of(c * tc, tc)
    for c0 in range(0, tc, sub):
        cols = slice(c0, c0 + sub)
        a = _dot(h, wa_ref[:, cols]) + ba_ref[:, cols]
        gt = _dot(h, wgt_ref[:, cols]) + bgt_ref[:, cols]
        u = a * jax.nn.sigmoid(gt)
        carry_sc[c, :, cols] = u[tm - hist:]
        for l0 in range(0, sub, LANES):
            sl = (c0 + l0) // LANES
            ext_sc[sl, :hist] = carry_in[:, c0 + l0:c0 + l0 + LANES]
            ext_sc[sl, hist:] = u[:, l0:l0 + LANES]
            _conv31_slab(ext_sc, sl, cw_ref, cb_ref, u_sc, col0)
    for c0 in range(0, tc, sub):
        cols = slice(c0, c0 + sub)
        o_ref[:, cols] = xlag_ref[:, cols] + _dot(act, w2_ref[:, cols]) + b2_ref[:, cols]


def _conformer_block(x, norm_g, w_pw1, b_pw1, conv_w, conv_b, ln_g, ln_b, w_pw2, b_pw2):
    b, s, d = x.shape
    taps = conv_w.shape[0]
    hist = -(-(taps - 1) // SUBLANES) * SUBLANES
    tm = _largest_tile(s, MAX_SEQ_TILE, CONV_ROW_BLOCK)
    tc = _largest_tile(d, MAX_CHANNEL_TILE, LANES)
    nc = d // tc
    tiles_per_seq = s // tm
    n_items = b * tiles_per_seq * nc
    assert tm % CONV_ROW_BLOCK == 0 and tm >= hist and tc % LANES == 0
    row = lambda v: v.reshape(1, -1)

    def in_tile(n):
        tile = jnp.minimum(n, n_items - 1) // nc
        return tile // tiles_per_seq, tile % tiles_per_seq

    def out_block(n):
        item = jnp.maximum(n - nc, 0)
        tile = item // nc
        return tile // tiles_per_seq, tile % tiles_per_seq, item % nc

    chunk = lambda n: n % nc
    return pl.pallas_call(
        functools.partial(_conformer_kernel, nc=nc, tiles_per_seq=tiles_per_seq,
                          n_items=n_items),
        out_shape=jax.ShapeDtypeStruct(x.shape, x.dtype),
        grid=(n_items + nc,),
        in_specs=[
            pl.BlockSpec((None, tm, d), lambda n: (*in_tile(n), 0)),
            pl.BlockSpec((None, tm, tc), out_block),
            pl.BlockSpec((1, d), lambda n: (0, 0)),
            pl.BlockSpec((d, tc), lambda n: (0, chunk(n))),
            pl.BlockSpec((d, tc), lambda n: (0, chunk(n) + nc)),
            pl.BlockSpec((1, tc), lambda n: (0, chunk(n))),
            pl.BlockSpec((1, tc), lambda n: (0, chunk(n) + nc)),
            pl.BlockSpec((taps, tc), lambda n: (0, chunk(n))),
            pl.BlockSpec((1, tc), lambda n: (0, chunk(n))),
            pl.BlockSpec((1, d), lambda n: (0, 0)),
            pl.BlockSpec((1, d), lambda n: (0, 0)),
            pl.BlockSpec((d, tc), lambda n: (0, chunk(n))),
            pl.BlockSpec((1, tc), lambda n: (0, chunk(n))),
        ],
        out_specs=pl.BlockSpec((None, tm, tc), out_block),
        scratch_shapes=[
            pltpu.VMEM((tm, d), w_pw1.dtype),
            pltpu.VMEM((tc // LANES, tm + hist, LANES), jnp.float32),
            pltpu.VMEM((tm, d), jnp.float32),
            pltpu.VMEM((tm, d), w_pw2.dtype),
            pltpu.VMEM((nc, hist, tc), jnp.float32),
        ],
        compiler_params=_compiler_params(1),
        name="conformer_mixer",
    )(x, x, row(norm_g), w_pw1, w_pw1, row(b_pw1), row(b_pw1), conv_w, row(conv_b),
      row(ln_g), row(ln_b), w_pw2, row(b_pw2))


def kernel(x, mix_norm_g, ffn_norm_g, a_w_in, a_conv_w, a_w_out, b_w_group, b_scale, c_w_pw1, c_b_pw1, c_conv_w, c_conv_b, c_ln_g, c_ln_b, c_w_pw2, c_b_pw2, f_w_gate, f_w_up, f_conv_w, f_conv_b, f_w_down, final_norm_g):
    mxu = MXU_DTYPE
    depth = mix_norm_g.shape[0]
    for i in range(depth):
        kind, j = i % N_MIXERS, i // N_MIXERS
        if kind == 0:
            x = _short_conv_block(x, mix_norm_g[i], a_w_in[j].astype(mxu), a_conv_w[j],
                                  a_w_out[j].astype(mxu))
        elif kind == 1:
            x = _pooling_block(x, mix_norm_g[i], b_w_group[j].astype(mxu), b_scale[j])
        else:
            x = _conformer_block(x, mix_norm_g[i], c_w_pw1[j].astype(mxu), c_b_pw1[j],
                                 c_conv_w[j], c_conv_b[j], c_ln_g[j], c_ln_b[j],
                                 c_w_pw2[j].astype(mxu), c_b_pw2[j])
        x = _ffn_block(x, ffn_norm_g[i], f_w_gate[i].astype(mxu), f_w_up[i].astype(mxu),
                       f_conv_w[i], f_conv_b[i], f_w_down[i].astype(mxu), final_norm_g,
                       final_norm=(i == depth - 1))
    return x
```
